```python
import math
import jax
import jax.numpy as jnp
from jax import lax
import numpy as np

D_MODEL = 4096
BATCH = 2
SEQ = 4096
DEPTH = 4

CTX_LEN = 256
GRID_W = 64
N_MIXERS = 3
HEAD_DIM = 128
ROPE_THETA = 10000.0
ROPE_FREQS = HEAD_DIM // 4
Q_BLOCK = 128
NORM_EPS = 1e-6
MOD_SCALE = 0.5
N_MOD = 6
DIFF_HEADS = D_MODEL // (2 * HEAD_DIM)
DIFF_QK_DIM = DIFF_HEADS * 2 * HEAD_DIM
DIFF_V_DIM = DIFF_HEADS * 2 * HEAD_DIM
FOURIER_GROUPS = 4
GQA_Q_HEADS = D_MODEL // HEAD_DIM
GQA_KV_HEADS = GQA_Q_HEADS // 4
GQA_Q_DIM = GQA_Q_HEADS * HEAD_DIM
GQA_KV_DIM = GQA_KV_HEADS * HEAD_DIM
FFN_DIM = 5632
N_EXPERTS = 8
TOP_K = 2
EXPERT_DIM = 1024

kernel_name = 'hybrid_diffattn_fourier_gqa_moe_dit'


def rms_norm(x, g):
    xf = x.astype(jnp.float32)
    y = xf * lax.rsqrt(jnp.mean(xf * xf, axis=-1, keepdims=True) + NORM_EPS)
    return (y * g.astype(jnp.float32)).astype(x.dtype)


def modulate(h, shift, scale):
    return h * (1.0 + scale) + shift


def axial_rope_tables(rows):
    r, col = jnp.meshgrid(jnp.arange(rows), jnp.arange(GRID_W), indexing='ij')
    pos = jnp.stack([r.reshape(-1), col.reshape(-1)], axis=-1).astype(jnp.float32)
    inv_freq = 1.0 / (ROPE_THETA ** (jnp.arange(ROPE_FREQS, dtype=jnp.float32) / ROPE_FREQS))
    ang = pos[:, :, None] * inv_freq
    return jnp.cos(ang), jnp.sin(ang)


def apply_rope(x, cos, sin):
    shp = x.shape
    xs = x.reshape(shp[0], shp[1], -1, 2, 2, ROPE_FREQS)
    a, b = xs[..., 0, :], xs[..., 1, :]
    cs = cos[:, None].astype(x.dtype)
    sn = sin[:, None].astype(x.dtype)
    return jnp.stack([a * cs - b * sn, b * cs + a * sn], axis=-2).reshape(shp)


def sweep_query_blocks(fn, q):
    b, n = q.shape[0], q.shape[1]
    nb = n // Q_BLOCK
    qb = jnp.moveaxis(q.reshape(b, nb, Q_BLOCK, *q.shape[2:]), 1, 0)
    out = jnp.moveaxis(lax.map(fn, qb), 0, 1)
    return out.reshape(b, n, *out.shape[3:])


def diff_attend(q, k, v, lam):
    def block(qb):
        s = jnp.einsum('bqhcd,bshcd->bhcqs', qb, k).astype(jnp.float32)
        p = jax.nn.softmax(s, axis=-1)
        a = p[:, :, 0] - lam * p[:, :, 1]
        return jnp.einsum('bhqs,bshe->bqhe', a.astype(v.dtype), v)
    return sweep_query_blocks(block, q)


def gqa_attend(q, k, v):
    def block(qb):
        s = jnp.einsum('bqkgd,bskd->bkgqs', qb, k).astype(jnp.float32)
        p = jax.nn.softmax(s, axis=-1).astype(v.dtype)
        return jnp.einsum('bkgqs,bskd->bqkgd', p, v)
    return sweep_query_blocks(block, q)


def diff_attention_mixer(h, hc, w_in, w_out, lam_vecs, subln_g, layer_idx, cos, sin, ctx_out):
    lam_init = 0.8 - 0.6 * math.exp(-0.3 * layer_idx)
    lv = lam_vecs.astype(jnp.float32)
    lam = jnp.exp(jnp.sum(lv[0] * lv[1])) - jnp.exp(jnp.sum(lv[2] * lv[3])) + lam_init
    scale = HEAD_DIM ** -0.5

    def heads_qk(t):
        return t.reshape(t.shape[0], t.shape[1], DIFF_HEADS, 2, HEAD_DIM)

    def heads_v(t):
        return t.reshape(t.shape[0], t.shape[1], DIFF_HEADS, 2 * HEAD_DIM)

    p = h @ w_in
    q = apply_rope(heads_qk(p[..., :DIFF_QK_DIM]), cos, sin) * scale
    k = apply_rope(heads_qk(p[..., DIFF_QK_DIM:2 * DIFF_QK_DIM]), cos, sin)
    v = heads_v(p[..., 2 * DIFF_QK_DIM:])
    pc = hc @ (w_in if ctx_out else w_in[:, DIFF_QK_DIM:])
    pc_kv = pc[..., DIFF_QK_DIM:] if ctx_out else pc
    kc = heads_qk(pc_kv[..., :DIFF_QK_DIM])
    vc = heads_v(pc_kv[..., DIFF_QK_DIM:])

    def finish(o):
        o = rms_norm(o, subln_g) * (1.0 - lam_init)
        return o.reshape(o.shape[0], o.shape[1], DIFF_V_DIM) @ w_out

    y = finish(diff_attend(q, jnp.concatenate([k, kc], axis=1), jnp.concatenate([v, vc], axis=1), lam))
    yc = finish(diff_attend(heads_qk(pc[..., :DIFF_QK_DIM]) * scale, kc, vc, lam)) if ctx_out else None
    return y, yc


def fourier_mixer(h, hc, w_out, ctx_out):
    def mix(z):
        b, n, _ = z.shape
        zg = z.astype(jnp.float32).reshape(b, n, FOURIER_GROUPS, D_MODEL // FOURIER_GROUPS)
        f = jnp.fft.fft2(zg, axes=(1, 3), norm='ortho').real
        return f.reshape(b, n, D_MODEL).astype(z.dtype) @ w_out
    return mix(h), (mix(hc) if ctx_out else None)


def gqa_mixer(h, hc, w_in, w_out, qk_g, cos, sin, ctx_out):
    grp = GQA_Q_HEADS // GQA_KV_HEADS
    scale = HEAD_DIM ** -0.5

    def heads_q(t):
        return rms_norm(t.reshape(t.shape[0], t.shape[1], GQA_KV_HEADS, grp, HEAD_DIM), qk_g[0])

    def heads_kv(t):
        return t.reshape(t.shape[0], t.shape[1], GQA_KV_HEADS, HEAD_DIM)

    p = h @ w_in
    q = apply_rope(heads_q(p[..., :GQA_Q_DIM]), cos, sin) * scale
    k = apply_rope(rms_norm(heads_kv(p[..., GQA_Q_DIM:GQA_Q_DIM + GQA_KV_DIM]), qk_g[1]), cos, sin)
    v = heads_kv(p[..., GQA_Q_DIM + GQA_KV_DIM:])
    pc = hc @ (w_in if ctx_out else w_in[:, GQA_Q_DIM:])
    pc_kv = pc[..., GQA_Q_DIM:] if ctx_out else pc
    kc = rms_norm(heads_kv(pc_kv[..., :GQA_KV_DIM]), qk_g[1])
    vc = heads_kv(pc_kv[..., GQA_KV_DIM:])

    def finish(o):
        return o.reshape(o.shape[0], o.shape[1], GQA_Q_DIM) @ w_out

    y = finish(gqa_attend(q, jnp.concatenate([k, kc], axis=1), jnp.concatenate([v, vc], axis=1)))
    yc = finish(gqa_attend(heads_q(pc[..., :GQA_Q_DIM]) * scale, kc, vc)) if ctx_out else None
    return y, yc


def swiglu(z, w_in, w_out):
    g, u = jnp.split(z @ w_in, 2, axis=-1)
    return (jax.nn.silu(g) * u) @ w_out


def moe_swiglu(z, w_router, w_in, w_out):
    logits = (z @ w_router).astype(jnp.float32)
    top_vals, top_idx = lax.top_k(logits, TOP_K)
    top_w = jax.nn.softmax(top_vals, axis=-1)
    gates = jnp.sum(jax.nn.one_hot(top_idx, N_EXPERTS, dtype=jnp.float32) * top_w[..., None], axis=-2)
    g, u = jnp.split(jnp.einsum('...d,edf->...ef', z, w_in), 2, axis=-1)
    act = jax.nn.silu(g) * u * gates[..., None].astype(z.dtype)
    return jnp.einsum('...ef,efd->...d', act, w_out)


def setup_inputs(seed: int = 0) -> dict:
    key = jax.random.key(seed)
    ks = jax.random.split(key, 24)
    d = D_MODEL
    n_a = len(range(0, DEPTH, N_MIXERS))
    n_b = len(range(1, DEPTH, N_MIXERS))
    n_c = len(range(2, DEPTH, N_MIXERS))
    n_dense = (DEPTH + 1) // 2
    n_moe = DEPTH // 2

    def w(k, shape, fan_in, s=1.0):
        return jax.random.normal(k, shape, jnp.float32) * (s * fan_in ** -0.5)

    def gain(k, shape):
        return 1.0 + 0.02 * jax.random.normal(k, shape, jnp.float32)

    return {
        'x': jax.random.normal(ks[0], (BATCH, SEQ, d), jnp.float32),
        'c': jax.random.normal(ks[1], (BATCH, d), jnp.float32),
        'ctx': jax.random.normal(ks[2], (BATCH, CTX_LEN, d), jnp.float32),
        'c_ctx': jax.random.normal(ks[3], (d,), jnp.float32),
        'w_mod': w(ks[4], (DEPTH, d, N_MOD * d), d, MOD_SCALE),
        'b_mod': 0.02 * jax.random.normal(ks[5], (DEPTH, N_MOD * d), jnp.float32),
        'norm_g': gain(ks[6], (DEPTH, 2, d)),
        'diff_w_in': w(ks[7], (n_a, d, 2 * DIFF_QK_DIM + DIFF_V_DIM), d),
        'diff_w_out': w(ks[8], (n_a, DIFF_V_DIM, d), DIFF_V_DIM),
        'diff_lambda': 0.1 * jax.random.normal(ks[9], (n_a, 4, HEAD_DIM), jnp.float32),
        'diff_subln_g': gain(ks[10], (n_a, 2 * HEAD_DIM)),
        'fourier_w_out': w(ks[11], (n_b, d, d), d),
        'gqa_w_in': w(ks[12], (n_c, d, GQA_Q_DIM + 2 * GQA_KV_DIM), d),
        'gqa_w_out': w(ks[13], (n_c, GQA_Q_DIM, d), GQA_Q_DIM),
        'gqa_qk_g': gain(ks[14], (n_c, 2, HEAD_DIM)),
        'ffn_w_in': w(ks[15], (n_dense, d, 2 * FFN_DIM), d),
        'ffn_w_out': w(ks[16], (n_dense, FFN_DIM, d), FFN_DIM),
        'moe_router': w(ks[17], (n_moe, d, N_EXPERTS), d),
        'moe_w_in': w(ks[18], (n_moe, N_EXPERTS, d, 2 * EXPERT_DIM), d),
        'moe_w_out': w(ks[19], (n_moe, N_EXPERTS, EXPERT_DIM, d), EXPERT_DIM),
        'final_g': gain(ks[20], (d,)),
    }


def reference(x, c, ctx, c_ctx, w_mod, b_mod, norm_g, diff_w_in, diff_w_out, diff_lambda, diff_subln_g,
              fourier_w_out, gqa_w_in, gqa_w_out, gqa_qk_g, ffn_w_in, ffn_w_out, moe_router, moe_w_in,
              moe_w_out, final_g):
    rows = x.shape[1] // GRID_W
    cos, sin = axial_rope_tables(rows)
    s_c = jax.nn.silu(c)
    s_cc = jax.nn.silu(c_ctx)
    xc = ctx
    for i in range(DEPTH):
        last = i == DEPTH - 1
        n_mod_ctx = 2 if last else N_MOD
        m = (s_c @ w_mod[i] + b_mod[i]).reshape(-1, N_MOD, 1, D_MODEL)
        mc = (s_cc @ w_mod[i][:, :n_mod_ctx * D_MODEL] + b_mod[i][:n_mod_ctx * D_MODEL]).reshape(n_mod_ctx, 1, D_MODEL)

        h = modulate(rms_norm(x, norm_g[i, 0]), m[:, 0], m[:, 1])
        hc = modulate(rms_norm(xc, norm_g[i, 0]), mc[0], mc[1])
        kind, j = i % N_MIXERS, i // N_MIXERS
        if kind == 0:
            y, yc = diff_attention_mixer(h, hc, diff_w_in[j], diff_w_out[j], diff_lambda[j], diff_subln_g[j],
                                         i, cos, sin, not last)
        elif kind == 1:
            y, yc = fourier_mixer(h, hc, fourier_w_out[j], not last)
        else:
            y, yc = gqa_mixer(h, hc, gqa_w_in[j], gqa_w_out[j], gqa_qk_g[j], cos, sin, not last)
        x = x + m[:, 2] * y

        f = i // 2
        if i % 2 == 0:
            ffn = functools_partial_dense(ffn_w_in[f], ffn_w_out[f])
        else:
            ffn = functools_partial_moe(moe_router[f], moe_w_in[f], moe_w_out[f])
        x = x + m[:, 5] * ffn(modulate(rms_norm(x, norm_g[i, 1]), m[:, 3], m[:, 4]))
        if not last:
            xc = xc + mc[2] * yc
            xc = xc + mc[5] * ffn(modulate(rms_norm(xc, norm_g[i, 1]), mc[3], mc[4]))
    return rms_norm(x, final_g)


def functools_partial_dense(w_in, w_out):
    return lambda z: swiglu(z, w_in, w_out)


def functools_partial_moe(w_router, w_in, w_out):
    return lambda z: moe_swiglu(z, w_router, w_in, w_out)
```

```python
import functools
import math

import jax
import jax.numpy as jnp
from jax import lax
from jax.experimental import pallas as pl
from jax.experimental.pallas import tpu as pltpu

HEAD_DIM = 128
GRID_W = 64
ROPE_THETA = 10000.0
ROPE_FREQS = HEAD_DIM // 4
NORM_EPS = 1e-6
N_MOD = 6
N_MIXERS = 3
FOURIER_GROUPS = 4
GQA_GROUP = 4

LANES = 128
VMEM_LIMIT_BYTES = 56 * 2**20
ROW_TILE = 512
NORM_ROW_TILE = 256
Q_TILE = 256
W_TILE_BYTES = 12 * 2**20

F32 = jnp.float32
BF16 = jnp.bfloat16


def _cparams(n_axes):
    return pltpu.CompilerParams(dimension_semantics=("arbitrary",) * n_axes,
                                vmem_limit_bytes=VMEM_LIMIT_BYTES)


def _silu(v):
    return v / (1.0 + jnp.exp(-v))


class _Dims:
    def __init__(self, x, ctx):
        self.b, self.seq, self.d = x.shape
        self.ctx = ctx.shape[1]
        self.m_lat = self.b * self.seq
        self.m_ctx = self.b * self.ctx
        self.m_all = self.m_lat + self.m_ctx
        g = math.gcd(self.seq, self.m_ctx)
        self.tm = min(ROW_TILE, g)
        self.tr = min(NORM_ROW_TILE, g)
        self.tq = min(Q_TILE, self.ctx)
        assert self.seq % self.tq == 0 and self.ctx % self.tq == 0

    def group(self, i, tile):
        r = i * tile
        return jnp.where(r < self.m_lat, r // self.seq, self.b)


def _mod_kernel(c_ref, w_ref, b_ref, o_ref):
    s = _silu(c_ref[...]).astype(BF16)
    w = w_ref[...].astype(BF16)
    o_ref[...] = jnp.dot(s, w, preferred_element_type=F32) + b_ref[...]


def _modulations(cc, w_mod, b_mod):
    depth, d, n = w_mod.shape
    tn = 512 if n % 512 == 0 else n
    rows = cc.shape[0]
    return pl.pallas_call(
        _mod_kernel,
        out_shape=jax.ShapeDtypeStruct((depth, rows, n), F32),
        grid=(depth, n // tn),
        in_specs=[pl.BlockSpec((rows, d), lambda l, j: (0, 0)),
                  pl.BlockSpec((None, d, tn), lambda l, j: (l, 0, j)),
                  pl.BlockSpec((None, 1, tn), lambda l, j: (l, 0, j))],
        out_specs=pl.BlockSpec((None, rows, tn), lambda l, j: (l, 0, j)),
        compiler_params=_cparams(2),
        name="adaln_modulations",
    )(cc, w_mod, b_mod.reshape(depth, 1, n))


def _norm_mod_value(x_ref, g_ref, mod_ref, ci):
    x = x_ref[...]
    y = x * lax.rsqrt(jnp.mean(x * x, axis=-1, keepdims=True) + NORM_EPS) * g_ref[...]
    shift = mod_ref[ci:ci + 1, :]
    scale = mod_ref[ci + 1:ci + 2, :]
    return y * (1.0 + scale) + shift


def _norm_mod_kernel(x_ref, g_ref, mod_ref, o_ref, *, ci):
    o_ref[...] = _norm_mod_value(x_ref, g_ref, mod_ref, ci).astype(BF16)


def _split_bf16(v):
    hi = v.astype(BF16)
    lo = (v - hi.astype(F32)).astype(BF16)
    return hi, lo


def _norm_mod_router_kernel(x_ref, g_ref, mod_ref, wr_ref, o_ref, gates_ref, *, ci):
    z = _norm_mod_value(x_ref, g_ref, mod_ref, ci)
    o_ref[...] = z.astype(BF16)
    z_hi, z_lo = _split_bf16(z)
    w_hi, w_lo = _split_bf16(wr_ref[...])
    logits = (jnp.dot(z_hi, w_hi, preferred_element_type=F32)
              + jnp.dot(z_lo, w_hi, preferred_element_type=F32)
              + jnp.dot(z_hi, w_lo, preferred_element_type=F32))
    n_e = gates_ref.shape[-1]
    idx = lax.broadcasted_iota(jnp.int32, logits.shape, 1).astype(F32)
    logits = jnp.where(idx < n_e, logits, -jnp.inf)
    m1 = jnp.max(logits, axis=-1, keepdims=True)
    i1 = jnp.min(jnp.where(logits == m1, idx, float(LANES)), axis=-1, keepdims=True)
    rest = jnp.where(idx == i1, -jnp.inf, logits)
    m2 = jnp.max(rest, axis=-1, keepdims=True)
    i2 = jnp.min(jnp.where(rest == m2, idx, float(LANES)), axis=-1, keepdims=True)
    e2 = jnp.exp(m2 - m1)
    w1 = 1.0 / (1.0 + e2)
    w2 = e2 * w1
    gates = jnp.where(idx == i1, w1, 0.0) + jnp.where(idx == i2, w2, 0.0)
    gates_ref[...] = gates[:, :n_e]


def _norm_mod(dm, x, g, mod, ci, rows, w_router=None):
    d = x.shape[1]
    tr = dm.tr
    in_specs = [pl.BlockSpec((tr, d), lambda i: (i, 0)),
                pl.BlockSpec((1, d), lambda i: (0, 0)),
                pl.BlockSpec((None, N_MOD, d), lambda i: (dm.group(i, tr), 0, 0))]
    h_spec = pl.BlockSpec((tr, d), lambda i: (i, 0))
    h_shape = jax.ShapeDtypeStruct((rows, d), BF16)
    if w_router is None:
        return pl.pallas_call(
            functools.partial(_norm_mod_kernel, ci=ci),
            out_shape=h_shape, grid=(rows // tr,), in_specs=in_specs, out_specs=h_spec,
            compiler_params=_cparams(1), name="norm_modulate",
        )(x, g.reshape(1, d), mod)
    n_e = w_router.shape[1]
    assert n_e <= LANES
    w_router = jnp.pad(w_router, ((0, 0), (0, LANES - n_e)))
    return pl.pallas_call(
        functools.partial(_norm_mod_router_kernel, ci=ci),
        out_shape=(h_shape, jax.ShapeDtypeStruct((rows, n_e), F32)),
        grid=(rows // tr,),
        in_specs=in_specs + [pl.BlockSpec((d, LANES), lambda i: (0, 0))],
        out_specs=(h_spec, pl.BlockSpec((tr, n_e), lambda i: (i, 0))),
        compiler_params=_cparams(1), name="norm_modulate_router",
    )(x, g.reshape(1, d), mod, w_router)


def _final_norm_kernel(x_ref, g_ref, o_ref):
    x = x_ref[...]
    o_ref[...] = x * lax.rsqrt(jnp.mean(x * x, axis=-1, keepdims=True) + NORM_EPS) * g_ref[...]


def _final_norm(dm, x, g):
    rows, d = x.shape
    tr = dm.tr
    return pl.pallas_call(
        _final_norm_kernel,
        out_shape=jax.ShapeDtypeStruct((rows, d), F32),
        grid=(rows // tr,),
        in_specs=[pl.BlockSpec((tr, d), lambda i: (i, 0)), pl.BlockSpec((1, d), lambda i: (0, 0))],
        out_specs=pl.BlockSpec((tr, d), lambda i: (i, 0)),
        compiler_params=_cparams(1), name="final_norm",
    )(x, g.reshape(1, d))


def _rope_epilogue(acc, cos_ref, sina_ref, sinb_ref, g_ref, o_ref, scale):
    cos, sina, sinb = cos_ref[...], sina_ref[...], sinb_ref[...]
    for c in range(acc.shape[1] // HEAD_DIM):
        xh = acc[:, c * HEAD_DIM:(c + 1) * HEAD_DIM]
        if g_ref is not None:
            xh = xh * lax.rsqrt(jnp.mean(xh * xh, axis=-1, keepdims=True) + NORM_EPS) * g_ref[...]
        fwd = pltpu.roll(xh, HEAD_DIM - ROPE_FREQS, 1)
        bwd = pltpu.roll(xh, ROPE_FREQS, 1)
        r = xh * cos + fwd * sina + bwd * sinb
        if scale != 1.0:
            r = r * scale
        o_ref[:, c * HEAD_DIM:(c + 1) * HEAD_DIM] = r.astype(o_ref.dtype)


def _mm_kernel(*refs, mode, n_w, cast, n_extra, gate_idx, scale, blocks_per_expert):
    a_ref = refs[0]
    w_refs = refs[1:1 + n_w]
    extra = refs[1 + n_w:1 + n_w + n_extra]
    o_ref = refs[1 + n_w + n_extra]
    wb_refs = refs[2 + n_w + n_extra:]

    if cast:
        @pl.when(pl.program_id(1) == 0)
        def _():
            for w_ref, wb_ref in zip(w_refs, wb_refs):
                wb_ref[...] = w_ref[...].astype(BF16)
        w_vals = [wb[...] for wb in wb_refs]
    else:
        w_vals = [w[...] for w in w_refs]

    a = a_ref[...]
    accs = [jnp.dot(a, w, preferred_element_type=F32) for w in w_vals]

    if mode == "plain":
        o_ref[...] = accs[0].astype(o_ref.dtype)
    elif mode == "rope":
        g_ref = extra[3] if n_extra == 4 else None
        _rope_epilogue(accs[0], extra[0], extra[1], extra[2], g_ref, o_ref, scale)
    elif mode == "resid":
        x_ref, mod_ref = extra
        gate = mod_ref[gate_idx:gate_idx + 1, :]
        o_ref[...] = x_ref[...] + gate * accs[0]
    elif mode == "swiglu":
        hdn = _silu(accs[0]) * accs[1]
        if n_extra == 1:
            gates = extra[0][...]
            e = pl.program_id(0) // blocks_per_expert
            sel = lax.broadcasted_iota(jnp.int32, gates.shape, 1) == e
            hdn = hdn * jnp.sum(jnp.where(sel, gates, 0.0), axis=-1, keepdims=True)
        o_ref[...] = hdn.astype(o_ref.dtype)
    else:
        raise ValueError(mode)


def _pick_tn(k, n, itemsize, n_w):
    best = None
    for tn in range(LANES, n + 1, LANES):
        if n % tn == 0 and k * tn * itemsize * n_w <= W_TILE_BYTES:
            best = tn
    return best if best is not None else (LANES if n % LANES == 0 else n)


def _matmul(dm, a, w, *, rows, mode, out_dtype, col0=0, n_cols=None, tn=None, tm=None,
            extra=(), extra_specs=(), gate_idx=0, scale=1.0, swiglu_half=None):
    if w.ndim == 2:
        w = w.reshape(1, *w.shape)
    n_e, k, n_w_cols = w.shape
    assert a.shape[1] == k
    cast = w.dtype != BF16
    n_w = 2 if mode == "swiglu" else 1
    if n_cols is None:
        n_cols = swiglu_half if mode == "swiglu" else n_w_cols - col0
    if tn is None:
        tn = _pick_tn(k, n_cols, w.dtype.itemsize, n_w)
    tm = tm or dm.tm
    assert n_cols % tn == 0 and col0 % tn == 0 and rows % tm == 0
    bpe = n_cols // tn
    blk0 = col0 // tn

    def w_map(off):
        return lambda j, i: (j // bpe, 0, blk0 + off + j % bpe)

    in_specs = [pl.BlockSpec((tm, k), lambda j, i: (i, 0)),
                pl.BlockSpec((None, k, tn), w_map(0))]
    args = [a, w]
    if n_w == 2:
        assert swiglu_half % tn == 0
        in_specs.append(pl.BlockSpec((None, k, tn), w_map(swiglu_half // tn)))
        args.append(w)
    in_specs += list(extra_specs)
    args += list(extra)
    scratch = [pltpu.VMEM((k, tn), BF16) for _ in range(n_w)] if cast else []
    kern = functools.partial(_mm_kernel, mode=mode, n_w=n_w, cast=cast, n_extra=len(extra),
                             gate_idx=gate_idx, scale=scale, blocks_per_expert=bpe)
    return pl.pallas_call(
        kern,
        out_shape=jax.ShapeDtypeStruct((rows, n_e * n_cols), out_dtype),
        grid=(n_e * bpe, rows // tm),
        in_specs=in_specs,
        out_specs=pl.BlockSpec((tm, tn), lambda j, i: (i, j)),
        scratch_shapes=scratch,
        compiler_params=_cparams(2),
        name="matmul_" + mode,
    )(*args)


def _proj_rope(dm, h, w, col0, n_cols, rows, tabs, scale=1.0, gain=None):
    tm = dm.tm
    tab_spec = pl.BlockSpec((tm, HEAD_DIM), lambda j, i: (i, 0))
    extra, specs = list(tabs), [tab_spec] * 3
    if gain is not None:
        extra.append(gain.reshape(1, HEAD_DIM))
        specs.append(pl.BlockSpec((1, HEAD_DIM), lambda j, i: (0, 0)))
    return _matmul(dm, h, w, rows=rows, mode="rope", out_dtype=BF16, col0=col0, n_cols=n_cols,
                   extra=extra, extra_specs=specs, scale=scale)


def _proj_resid(dm, a, w, x, mod, gate_idx, rows):
    tm = dm.tm
    k = w.shape[-2]
    n = w.shape[-1]
    tn = _pick_tn(k, n, w.dtype.itemsize, 1)
    specs = [pl.BlockSpec((tm, tn), lambda j, i: (i, j)),
             pl.BlockSpec((None, N_MOD, tn), lambda j, i: (dm.group(i, tm), 0, j))]
    return _matmul(dm, a, w, rows=rows, mode="resid", out_dtype=F32, tn=tn,
                   extra=[x, mod], extra_specs=specs, gate_idx=gate_idx)


_NT = (((1,), (1,)), ((), ()))


def _softmax_parts(q, k_refs, lo, hi):
    s = [lax.dot_general(q, k_ref[:, lo:hi], _NT, preferred_element_type=F32) for k_ref in k_refs]
    m = functools.reduce(jnp.maximum, [jnp.max(v, axis=-1, keepdims=True) for v in s])
    e = [jnp.exp(v - m) for v in s]
    l = functools.reduce(lambda u, v: u + v, [jnp.sum(v, axis=-1, keepdims=True) for v in e])
    return e, 1.0 / l


def _diff_attn_body(q_ref, k_refs, v_refs, lam_ref, g_ref, o_ref, lam_init):
    lv = lam_ref[...]
    lam = (jnp.exp(jnp.sum(lv[0:1] * lv[1:2], axis=-1, keepdims=True))
           - jnp.exp(jnp.sum(lv[2:3] * lv[3:4], axis=-1, keepdims=True)) + lam_init)
    e0, r0 = _softmax_parts(q_ref[:, :HEAD_DIM], k_refs, 0, HEAD_DIM)
    e1, r1 = _softmax_parts(q_ref[:, HEAD_DIM:], k_refs, HEAD_DIM, 2 * HEAD_DIM)
    r1 = r1 * lam
    o = None
    for u0, u1, v_ref in zip(e0, e1, v_refs):
        a = (u0 * r0 - u1 * r1).astype(BF16)
        pv = jnp.dot(a, v_ref[...], preferred_element_type=F32)
        o = pv if o is None else o + pv
    o = o * lax.rsqrt(jnp.mean(o * o, axis=-1, keepdims=True) + NORM_EPS) * g_ref[...]
    o_ref[...] = (o * (1.0 - lam_init)).astype(o_ref.dtype)


def _gqa_attn_body(q_ref, k_refs, v_refs, o_ref):
    for g in range(GQA_GROUP):
        lo, hi = g * HEAD_DIM, (g + 1) * HEAD_DIM
        e, r = _softmax_parts(q_ref[:, lo:hi], k_refs, 0, HEAD_DIM)
        o = None
        for u, v_ref in zip(e, v_refs):
            pv = jnp.dot((u * r).astype(BF16), v_ref[...], preferred_element_type=F32)
            o = pv if o is None else o + pv
        o_ref[:, lo:hi] = o.astype(o_ref.dtype)


def _attn_kernel(q_ref, kl_ref, kc_ref, vl_ref, vc_ref, *rest, kind, n_lat_tiles, ctx_queries, lam_init):
    o_ref = rest[-1]
    params = rest[:-1]

    def run(k_refs, v_refs):
        if kind == "diff":
            _diff_attn_body(q_ref, k_refs, v_refs, params[0], params[1], o_ref, lam_init)
        else:
            _gqa_attn_body(q_ref, k_refs, v_refs, o_ref)

    if not ctx_queries:
        run((kl_ref, kc_ref), (vl_ref, vc_ref))
        return

    qi = pl.program_id(2)

    @pl.when(qi < n_lat_tiles)
    def _():
        run((kl_ref, kc_ref), (vl_ref, vc_ref))

    @pl.when(qi >= n_lat_tiles)
    def _():
        run((kc_ref,), (vc_ref,))


def _attention(dm, q, k, v, *, kind, ctx_out, params=(), lam_init=0.0):
    tq = dm.tq
    n_lat = dm.seq // tq
    n_q = n_lat + (dm.ctx // tq if ctx_out else 0)
    wq = 2 * HEAD_DIM if kind == "diff" else GQA_GROUP * HEAD_DIM
    wk = 2 * HEAD_DIM if kind == "diff" else HEAD_DIM
    n_heads = k.shape[1] // wk
    lat_per_ctx = dm.m_lat // dm.ctx
    ctx_tiles = dm.ctx // tq

    def q_row(b, qi):
        return jnp.where(qi < n_lat, b * n_lat + qi, dm.m_lat // tq + b * ctx_tiles + (qi - n_lat))

    q_spec = pl.BlockSpec((tq, wq), lambda b, h, qi: (q_row(b, qi), h))
    lat_spec = lambda w: pl.BlockSpec((dm.seq, w), lambda b, h, qi: (b, h))
    ctx_spec = lambda w: pl.BlockSpec((dm.ctx, w), lambda b, h, qi: (lat_per_ctx + b, h))
    p_specs = [pl.BlockSpec(p.shape, lambda b, h, qi: (0, 0)) for p in params]
    rows = dm.m_all if ctx_out else dm.m_lat
    return pl.pallas_call(
        functools.partial(_attn_kernel, kind=kind, n_lat_tiles=n_lat, ctx_queries=ctx_out,
                          lam_init=lam_init),
        out_shape=jax.ShapeDtypeStruct((rows, q.shape[1]), BF16),
        grid=(dm.b, n_heads, n_q),
        in_specs=[q_spec, lat_spec(wk), ctx_spec(wk), lat_spec(wk), ctx_spec(wk)] + p_specs,
        out_specs=q_spec,
        compiler_params=_cparams(3),
        name=kind + "_attention",
    )(q, k, k, v, v, *params)


def _dft_tables(n, norm):
    n0 = 1
    while n0 * n0 * 4 <= n and n % (n0 * 2) == 0:
        n0 *= 2
    n1 = n // n0
    k = jnp.arange(n, dtype=jnp.int32)[:, None]

    def cs(m):
        ang = ((k * m) % n).astype(F32) * (2.0 * math.pi / n)
        return jnp.cos(ang), jnp.sin(ang)

    c1, s1 = cs(jnp.arange(n1, dtype=jnp.int32)[None, :] * n0)
    c0, s0 = cs(jnp.arange(n0, dtype=jnp.int32)[None, :])
    c1, s1, c0, s0 = c1[:, :, None], s1[:, :, None], c0[:, None, :], s0[:, None, :]
    cos = (c1 * c0 - s1 * s0).reshape(n, n) * norm
    sin = (s1 * c0 + c1 * s0).reshape(n, n) * norm
    return cos, sin


def _dft_rows_kernel(c_ref, s_ref, yc_ref, ys_ref, *rest):
    o_ref = rest[-1]
    o_ref[...] = (jnp.dot(c_ref[...], yc_ref[...], preferred_element_type=F32)
                  + jnp.dot(s_ref[...], ys_ref[...], preferred_element_type=F32)).astype(o_ref.dtype)


def _dft_rows(dm, y, n, row0, out_rows, prev=None):
    d = dm.d
    dg = d // FOURIER_GROUPS
    cos, sin = _dft_tables(n, n ** -0.5)
    cos, nsin = cos.astype(BF16), (-sin).astype(BF16)
    tmf = min(ROW_TILE, n)
    tn = min(512, dg)
    lb = dg // tn
    rb0 = row0 // n

    def y_map(off):
        return lambda b, j, i: (rb0 + b, (j // lb) * 2 * lb + off + j % lb)

    in_specs = [pl.BlockSpec((tmf, n), lambda b, j, i: (i, 0)),
                pl.BlockSpec((tmf, n), lambda b, j, i: (i, 0)),
                pl.BlockSpec((n, tn), y_map(0)),
                pl.BlockSpec((n, tn), y_map(lb))]
    args = [cos, nsin, y, y]
    aliases = {}
    if prev is not None:
        in_specs.append(pl.BlockSpec(memory_space=pl.ANY))
        args.append(prev)
        aliases = {4: 0}
    tiles = n // tmf
    return pl.pallas_call(
        _dft_rows_kernel,
        out_shape=jax.ShapeDtypeStruct((out_rows, d), BF16),
        grid=(dm.b, d // tn, tiles),
        in_specs=in_specs,
        out_specs=pl.BlockSpec((tmf, tn), lambda b, j, i: ((rb0 + b) * tiles + i, j)),
        input_output_aliases=aliases,
        compiler_params=_cparams(3),
        name="dft_positions",
    )(*args)


def _fourier_mix(dm, h, ctx_out):
    d = dm.d
    dg = d // FOURIER_GROUPS
    rows = h.shape[0]
    cos_c, sin_c = _dft_tables(dg, dg ** -0.5)
    cs = jnp.concatenate([cos_c, sin_c], axis=1).astype(BF16)
    y = _matmul(dm, h.reshape(rows * FOURIER_GROUPS, dg), cs, rows=rows * FOURIER_GROUPS,
                mode="plain", out_dtype=BF16, tn=min(2 * dg, 1024))
    y = y.reshape(rows, 2 * d)
    out_rows = dm.m_all if ctx_out else dm.m_lat
    f = _dft_rows(dm, y, dm.seq, 0, out_rows)
    if ctx_out:
        f = _dft_rows(dm, y, dm.ctx, dm.m_lat, out_rows, prev=f)
    return f


def _rope_tables(dm):
    rows = dm.seq // GRID_W
    r, col = jnp.meshgrid(jnp.arange(rows), jnp.arange(GRID_W), indexing="ij")
    pos = jnp.stack([r.reshape(-1), col.reshape(-1)], axis=-1).astype(F32)
    inv_freq = 1.0 / (ROPE_THETA ** (jnp.arange(ROPE_FREQS, dtype=F32) / ROPE_FREQS))
    ang = pos[:, :, None] * inv_freq
    cos, sin = jnp.cos(ang), jnp.sin(ang)
    zero = jnp.zeros_like(sin)
    cos_t = jnp.stack([cos, cos], axis=2).reshape(dm.seq, HEAD_DIM)
    sina_t = jnp.stack([-sin, zero], axis=2).reshape(dm.seq, HEAD_DIM)
    sinb_t = jnp.stack([zero, sin], axis=2).reshape(dm.seq, HEAD_DIM)

    def full(t, fill):
        return jnp.concatenate([jnp.tile(t, (dm.b, 1)), jnp.full((dm.m_ctx, HEAD_DIM), fill, F32)], axis=0)

    return full(cos_t, 1.0), full(sina_t, 0.0), full(sinb_t, 0.0)


def kernel(x, c, ctx, c_ctx, w_mod, b_mod, norm_g, diff_w_in, diff_w_out, diff_lambda, diff_subln_g,
           fourier_w_out, gqa_w_in, gqa_w_out, gqa_qk_g, ffn_w_in, ffn_w_out, moe_router, moe_w_in,
           moe_w_out, final_g):
    dm = _Dims(x, ctx)
    b, d = dm.b, dm.d
    depth = w_mod.shape[0]
    scale = HEAD_DIM ** -0.5

    cc = jnp.concatenate([c, c_ctx[None, :], jnp.zeros((8 - b - 1, d), F32)], axis=0)
    mods = _modulations(cc, w_mod, b_mod)[:, :b + 1].reshape(depth, b + 1, N_MOD, d)
    tabs = _rope_tables(dm)
    xs = jnp.concatenate([x.reshape(dm.m_lat, d), ctx.reshape(dm.m_ctx, d)], axis=0)

    for i in range(depth):
        last = i == depth - 1
        mod = mods[i]
        rows_in = xs.shape[0]
        rows_out = dm.m_lat if last else dm.m_all
        h = _norm_mod(dm, xs, norm_g[i, 0], mod, 0, rows_in)
        kind, j = i % N_MIXERS, i // N_MIXERS
        if kind == 0:
            w_in = diff_w_in[j]
            qk = w_in.shape[1] // 3
            q = _proj_rope(dm, h, w_in, 0, qk, rows_out, tabs, scale=scale)
            k = _proj_rope(dm, h, w_in, qk, qk, rows_in, tabs)
            v = _matmul(dm, h, w_in, rows=rows_in, mode="plain", out_dtype=BF16, col0=2 * qk, n_cols=qk)
            lam_init = 0.8 - 0.6 * math.exp(-0.3 * i)
            y = _attention(dm, q, k, v, kind="diff", ctx_out=not last,
                           params=(diff_lambda[j], diff_subln_g[j].reshape(1, 2 * HEAD_DIM)),
                           lam_init=lam_init)
            w_out = diff_w_out[j]
        elif kind == 1:
            y = _fourier_mix(dm, h, not last)
            w_out = fourier_w_out[j]
        else:
            w_in = gqa_w_in[j]
            kvd = (w_in.shape[1] - d) // 2
            q = _proj_rope(dm, h, w_in, 0, d, rows_out, tabs, scale=scale, gain=gqa_qk_g[j, 0])
            k = _proj_rope(dm, h, w_in, d, kvd, rows_in, tabs, gain=gqa_qk_g[j, 1])
            v = _matmul(dm, h, w_in, rows=rows_in, mode="plain", out_dtype=BF16, col0=d + kvd, n_cols=kvd)
            y = _attention(dm, q, k, v, kind="gqa", ctx_out=not last)
            w_out = gqa_w_out[j]
        xs = _proj_resid(dm, y, w_out, xs, mod, 2, rows_out)

        f = i // 2
        if i % 2 == 0:
            z = _norm_mod(dm, xs, norm_g[i, 1], mod, 3, rows_out)
            w_in = ffn_w_in[f]
            hdn = _matmul(dm, z, w_in, rows=rows_out, mode="swiglu", out_dtype=BF16,
                          swiglu_half=w_in.shape[1] // 2)
            xs = _proj_resid(dm, hdn, ffn_w_out[f], xs, mod, 5, rows_out)
        else:
            z, gates = _norm_mod(dm, xs, norm_g[i, 1], mod, 3, rows_out, w_router=moe_router[f])
            w_in = moe_w_in[f]
            n_e, _, two_f = w_in.shape
            gate_spec = pl.BlockSpec((dm.tm, n_e), lambda jj, ii: (ii, 0))
            hdn = _matmul(dm, z, w_in, rows=rows_out, mode="swiglu", out_dtype=BF16,
                          swiglu_half=two_f // 2, extra=[gates], extra_specs=[gate_spec])
            w_out = moe_w_out[f].reshape(n_e * (two_f // 2), d).astype(BF16)
            xs = _proj_resid(dm, hdn, w_out, xs, mod, 5, rows_out)

    return _final_norm(dm, xs, final_g).reshape(b, dm.seq, d)
```

```python
import functools
import math

import jax
import jax.numpy as jnp
from jax import lax
from jax.experimental import pallas as pl
from jax.experimental.pallas import tpu as pltpu

HEAD_DIM = 128
GRID_W = 64
ROPE_THETA = 10000.0
ROPE_FREQS = HEAD_DIM // 4
NORM_EPS = 1e-6
N_MOD = 6
N_MIXERS = 3
FOURIER_GROUPS = 4
GQA_GROUP = 4

LANES = 128
VMEM_LIMIT_BYTES = 56 * 2**20
ROW_TILE = 512
NORM_ROW_TILE = 256
Q_TILE = 256
DIFF_SUBTILES = 2
KEY_CHUNK = 512
LOG2E = math.log2(math.e)
W_TILE_BYTES = 12 * 2**20

F32 = jnp.float32
BF16 = jnp.bfloat16


def _cparams(n_axes):
    return pltpu.CompilerParams(dimension_semantics=("arbitrary",) * n_axes,
                                vmem_limit_bytes=VMEM_LIMIT_BYTES)


def _silu(v):
    return v / (1.0 + jnp.exp(-v))


class _Dims:
    def __init__(self, x, ctx):
        self.b, self.seq, self.d = x.shape
        self.ctx = ctx.shape[1]
        self.m_lat = self.b * self.seq
        self.m_ctx = self.b * self.ctx
        self.m_all = self.m_lat + self.m_ctx
        g = math.gcd(self.seq, self.m_ctx)
        self.tm = min(ROW_TILE, g)
        self.tr = min(NORM_ROW_TILE, g)
        self.tq = min(Q_TILE, self.ctx)
        assert self.seq % self.tq == 0 and self.ctx % self.tq == 0

    def group(self, i, tile):
        r = i * tile
        return jnp.where(r < self.m_lat, r // self.seq, self.b)


def _mod_kernel(c_ref, w_ref, b_ref, o_ref):
    s = _silu(c_ref[...]).astype(BF16)
    w = w_ref[...].astype(BF16)
    o_ref[...] = jnp.dot(s, w, preferred_element_type=F32) + b_ref[...]


def _modulations(cc, w_mod, b_mod):
    depth, d, n = w_mod.shape
    tn = 512 if n % 512 == 0 else n
    rows = cc.shape[0]
    return pl.pallas_call(
        _mod_kernel,
        out_shape=jax.ShapeDtypeStruct((depth, rows, n), F32),
        grid=(depth, n // tn),
        in_specs=[pl.BlockSpec((rows, d), lambda l, j: (0, 0)),
                  pl.BlockSpec((None, d, tn), lambda l, j: (l, 0, j)),
                  pl.BlockSpec((None, 1, tn), lambda l, j: (l, 0, j))],
        out_specs=pl.BlockSpec((None, rows, tn), lambda l, j: (l, 0, j)),
        compiler_params=_cparams(2),
        name="adaln_modulations",
    )(cc, w_mod, b_mod.reshape(depth, 1, n))


def _norm_mod_value(x_ref, g_ref, mod_ref, ci):
    x = x_ref[...]
    y = x * lax.rsqrt(jnp.mean(x * x, axis=-1, keepdims=True) + NORM_EPS) * g_ref[...]
    shift = mod_ref[ci:ci + 1, :]
    scale = mod_ref[ci + 1:ci + 2, :]
    return y * (1.0 + scale) + shift


def _norm_mod_kernel(x_ref, g_ref, mod_ref, o_ref, *, ci):
    o_ref[...] = _norm_mod_value(x_ref, g_ref, mod_ref, ci).astype(BF16)


def _split_bf16(v):
    hi = v.astype(BF16)
    lo = (v - hi.astype(F32)).astype(BF16)
    return hi, lo


def _norm_mod_router_kernel(x_ref, g_ref, mod_ref, wr_ref, o_ref, gates_ref, *, ci):
    z = _norm_mod_value(x_ref, g_ref, mod_ref, ci)
    o_ref[...] = z.astype(BF16)
    z_hi, z_lo = _split_bf16(z)
    w_hi, w_lo = _split_bf16(wr_ref[...])
    logits = (jnp.dot(z_hi, w_hi, preferred_element_type=F32)
              + jnp.dot(z_lo, w_hi, preferred_element_type=F32)
              + jnp.dot(z_hi, w_lo, preferred_element_type=F32))
    n_e = gates_ref.shape[-1]
    idx = lax.broadcasted_iota(jnp.int32, logits.shape, 1).astype(F32)
    logits = jnp.where(idx < n_e, logits, -jnp.inf)
    m1 = jnp.max(logits, axis=-1, keepdims=True)
    i1 = jnp.min(jnp.where(logits == m1, idx, float(LANES)), axis=-1, keepdims=True)
    rest = jnp.where(idx == i1, -jnp.inf, logits)
    m2 = jnp.max(rest, axis=-1, keepdims=True)
    i2 = jnp.min(jnp.where(rest == m2, idx, float(LANES)), axis=-1, keepdims=True)
    e2 = jnp.exp(m2 - m1)
    w1 = 1.0 / (1.0 + e2)
    w2 = e2 * w1
    gates = jnp.where(idx == i1, w1, 0.0) + jnp.where(idx == i2, w2, 0.0)
    gates_ref[...] = gates[:, :n_e]


def _norm_mod(dm, x, g, mod, ci, rows, w_router=None):
    d = x.shape[1]
    tr = dm.tr
    in_specs = [pl.BlockSpec((tr, d), lambda i: (i, 0)),
                pl.BlockSpec((1, d), lambda i: (0, 0)),
                pl.BlockSpec((None, N_MOD, d), lambda i: (dm.group(i, tr), 0, 0))]
    h_spec = pl.BlockSpec((tr, d), lambda i: (i, 0))
    h_shape = jax.ShapeDtypeStruct((rows, d), BF16)
    if w_router is None:
        return pl.pallas_call(
            functools.partial(_norm_mod_kernel, ci=ci),
            out_shape=h_shape, grid=(rows // tr,), in_specs=in_specs, out_specs=h_spec,
            compiler_params=_cparams(1), name="norm_modulate",
        )(x, g.reshape(1, d), mod)
    n_e = w_router.shape[1]
    assert n_e <= LANES
    w_router = jnp.pad(w_router, ((0, 0), (0, LANES - n_e)))
    return pl.pallas_call(
        functools.partial(_norm_mod_router_kernel, ci=ci),
        out_shape=(h_shape, jax.ShapeDtypeStruct((rows, n_e), F32)),
        grid=(rows // tr,),
        in_specs=in_specs + [pl.BlockSpec((d, LANES), lambda i: (0, 0))],
        out_specs=(h_spec, pl.BlockSpec((tr, n_e), lambda i: (i, 0))),
        compiler_params=_cparams(1), name="norm_modulate_router",
    )(x, g.reshape(1, d), mod, w_router)


def _final_norm_kernel(x_ref, g_ref, o_ref):
    x = x_ref[...]
    o_ref[...] = x * lax.rsqrt(jnp.mean(x * x, axis=-1, keepdims=True) + NORM_EPS) * g_ref[...]


def _final_norm(dm, x, g):
    rows, d = x.shape
    tr = dm.tr
    return pl.pallas_call(
        _final_norm_kernel,
        out_shape=jax.ShapeDtypeStruct((rows, d), F32),
        grid=(rows // tr,),
        in_specs=[pl.BlockSpec((tr, d), lambda i: (i, 0)), pl.BlockSpec((1, d), lambda i: (0, 0))],
        out_specs=pl.BlockSpec((tr, d), lambda i: (i, 0)),
        compiler_params=_cparams(1), name="final_norm",
    )(x, g.reshape(1, d))


def _rope_epilogue(acc, cos_ref, sina_ref, sinb_ref, g_ref, o_ref, scale):
    cos, sina, sinb = cos_ref[...], sina_ref[...], sinb_ref[...]
    for c in range(acc.shape[1] // HEAD_DIM):
        xh = acc[:, c * HEAD_DIM:(c + 1) * HEAD_DIM]
        if g_ref is not None:
            xh = xh * lax.rsqrt(jnp.mean(xh * xh, axis=-1, keepdims=True) + NORM_EPS) * g_ref[...]
        fwd = pltpu.roll(xh, HEAD_DIM - ROPE_FREQS, 1)
        bwd = pltpu.roll(xh, ROPE_FREQS, 1)
        r = xh * cos + fwd * sina + bwd * sinb
        if scale != 1.0:
            r = r * scale
        o_ref[:, c * HEAD_DIM:(c + 1) * HEAD_DIM] = r.astype(o_ref.dtype)


def _mm_kernel(*refs, mode, n_w, cast, n_extra, gate_idx, scale, blocks_per_expert):
    a_ref = refs[0]
    w_refs = refs[1:1 + n_w]
    extra = refs[1 + n_w:1 + n_w + n_extra]
    o_ref = refs[1 + n_w + n_extra]
    wb_refs = refs[2 + n_w + n_extra:]

    if cast:
        @pl.when(pl.program_id(1) == 0)
        def _():
            for w_ref, wb_ref in zip(w_refs, wb_refs):
                wb_ref[...] = w_ref[...].astype(BF16)
        w_vals = [wb[...] for wb in wb_refs]
    else:
        w_vals = [w[...] for w in w_refs]

    a = a_ref[...]
    accs = [jnp.dot(a, w, preferred_element_type=F32) for w in w_vals]

    if mode == "plain":
        o_ref[...] = accs[0].astype(o_ref.dtype)
    elif mode == "rope":
        g_ref = extra[3] if n_extra == 4 else None
        _rope_epilogue(accs[0], extra[0], extra[1], extra[2], g_ref, o_ref, scale)
    elif mode == "resid":
        x_ref, mod_ref = extra
        gate = mod_ref[gate_idx:gate_idx + 1, :]
        o_ref[...] = x_ref[...] + gate * accs[0]
    elif mode == "swiglu":
        hdn = _silu(accs[0]) * accs[1]
        if n_extra == 1:
            gates = extra[0][...]
            e = pl.program_id(0) // blocks_per_expert
            sel = lax.broadcasted_iota(jnp.int32, gates.shape, 1) == e
            hdn = hdn * jnp.sum(jnp.where(sel, gates, 0.0), axis=-1, keepdims=True)
        o_ref[...] = hdn.astype(o_ref.dtype)
    else:
        raise ValueError(mode)


def _pick_tn(k, n, itemsize, n_w):
    best = None
    for tn in range(LANES, n + 1, LANES):
        if n % tn == 0 and k * tn * itemsize * n_w <= W_TILE_BYTES:
            best = tn
    return best if best is not None else (LANES if n % LANES == 0 else n)


def _matmul(dm, a, w, *, rows, mode, out_dtype, e0=0, n_e=1, col0=0, n_cols=None, tn=None,
            extra=(), extra_specs=(), gate_idx=0, scale=1.0, swiglu_half=None, a_groups=False):
    _, k, n_w_cols = w.shape
    assert a.shape[1] == (n_e * k if a_groups else k)
    cast = w.dtype != BF16
    n_w = 2 if mode == "swiglu" else 1
    if n_cols is None:
        n_cols = swiglu_half if mode == "swiglu" else n_w_cols - col0
    if tn is None:
        tn = _pick_tn(k, n_cols, w.dtype.itemsize, n_w)
    tm = dm.tm
    assert n_cols % tn == 0 and col0 % tn == 0 and rows % tm == 0
    bpe = n_cols // tn
    blk0 = col0 // tn

    def w_map(off):
        if a_groups:
            return lambda j, i: (e0, 0, blk0 + off + j % bpe)
        return lambda j, i: (e0 + j // bpe, 0, blk0 + off + j % bpe)

    a_map = (lambda j, i: (i, j // bpe)) if a_groups else (lambda j, i: (i, 0))
    in_specs = [pl.BlockSpec((tm, k), a_map),
                pl.BlockSpec((None, k, tn), w_map(0))]
    args = [a, w]
    if n_w == 2:
        assert swiglu_half % tn == 0
        in_specs.append(pl.BlockSpec((None, k, tn), w_map(swiglu_half // tn)))
        args.append(w)
    in_specs += list(extra_specs)
    args += list(extra)
    scratch = [pltpu.VMEM((k, tn), BF16) for _ in range(n_w)] if cast else []
    kern = functools.partial(_mm_kernel, mode=mode, n_w=n_w, cast=cast, n_extra=len(extra),
                             gate_idx=gate_idx, scale=scale, blocks_per_expert=bpe)
    return pl.pallas_call(
        kern,
        out_shape=jax.ShapeDtypeStruct((rows, n_e * n_cols), out_dtype),
        grid=(n_e * bpe, rows // tm),
        in_specs=in_specs,
        out_specs=pl.BlockSpec((tm, tn), lambda j, i: (i, j)),
        scratch_shapes=scratch,
        compiler_params=_cparams(2),
        name="matmul_" + mode,
    )(*args)


def _proj_rope(dm, h, w, layer, col0, n_cols, rows, tabs, scale=1.0, gain=None):
    tm = dm.tm
    tab_spec = pl.BlockSpec((tm, HEAD_DIM), lambda j, i: (i, 0))
    extra, specs = list(tabs), [tab_spec] * 3
    if gain is not None:
        extra.append(gain.reshape(1, HEAD_DIM))
        specs.append(pl.BlockSpec((1, HEAD_DIM), lambda j, i: (0, 0)))
    return _matmul(dm, h, w, rows=rows, mode="rope", out_dtype=BF16, e0=layer, col0=col0,
                   n_cols=n_cols, extra=extra, extra_specs=specs, scale=scale)


def _proj_plain(dm, h, w, layer, col0, n_cols, rows):
    return _matmul(dm, h, w, rows=rows, mode="plain", out_dtype=BF16, e0=layer, col0=col0,
                   n_cols=n_cols)


def _proj_resid(dm, a, w, layer, x, mod, gate_idx, rows):
    tm = dm.tm
    _, k, n = w.shape
    tn = _pick_tn(k, n, w.dtype.itemsize, 1)
    specs = [pl.BlockSpec((tm, tn), lambda j, i: (i, j)),
             pl.BlockSpec((None, N_MOD, tn), lambda j, i: (dm.group(i, tm), 0, j))]
    return _matmul(dm, a, w, rows=rows, mode="resid", out_dtype=F32, e0=layer, tn=tn,
                   extra=[x, mod], extra_specs=specs, gate_idx=gate_idx)


_NT = (((1,), (1,)), ((), ()))


def _key_chunks(refs):
    out, off = [], 0
    for r in refs:
        n = r.shape[0]
        ck = min(KEY_CHUNK, n)
        for r0 in range(0, n, ck):
            out.append((r, r0, ck, off))
            off += ck
    return out


def _lane_fold(acc, v, op):
    for t in range(v.shape[1] // LANES):
        piece = v[:, t * LANES:(t + 1) * LANES]
        acc = piece if acc is None else op(acc, piece)
    return acc


def _scores_pass(q, k_refs, lo, hi, s_ref):
    mx = None
    for r, r0, ck, off in _key_chunks(k_refs):
        s = lax.dot_general(q, r[r0:r0 + ck, lo:hi], _NT, preferred_element_type=F32)
        s_ref[:, off:off + ck] = s
        mx = _lane_fold(mx, s, jnp.maximum)
    return jnp.max(mx, axis=-1, keepdims=True)


def _values_pass(s_ref, m, v_refs, lo, hi):
    acc, ls = None, None
    for r, r0, ck, off in _key_chunks(v_refs):
        e = jnp.exp2(s_ref[:, off:off + ck] - m)
        ls = _lane_fold(ls, e, jnp.add)
        pv = jnp.dot(e.astype(BF16), r[r0:r0 + ck, lo:hi], preferred_element_type=F32)
        acc = pv if acc is None else acc + pv
    return acc * (1.0 / jnp.sum(ls, axis=-1, keepdims=True))


def _diff_attn_body(q_ref, k_refs, v_refs, lam_ref, g_ref, o_ref, s_refs, sub, lam_init):
    lv = lam_ref[...]
    lam = (jnp.exp(jnp.sum(lv[0:1] * lv[1:2], axis=-1, keepdims=True))
           - jnp.exp(jnp.sum(lv[2:3] * lv[3:4], axis=-1, keepdims=True)) + lam_init)
    n_sub = q_ref.shape[0] // sub
    chains = [(t, c) for t in range(n_sub) for c in range(2)]
    m = {}
    for t, c in chains:
        lo, hi = c * HEAD_DIM, (c + 1) * HEAD_DIM
        m[t, c] = _scores_pass(q_ref[t * sub:(t + 1) * sub, lo:hi], k_refs, lo, hi, s_refs[2 * t + c])
    outs = {ch: _values_pass(s_refs[2 * ch[0] + ch[1]], m[ch], v_refs, 0, 2 * HEAD_DIM) for ch in chains}
    for t in range(n_sub):
        o = outs[t, 0] - lam * outs[t, 1]
        o = o * lax.rsqrt(jnp.mean(o * o, axis=-1, keepdims=True) + NORM_EPS) * g_ref[...]
        o_ref[t * sub:(t + 1) * sub, :] = (o * (1.0 - lam_init)).astype(o_ref.dtype)


def _gqa_attn_body(q_ref, k_refs, v_refs, o_ref, s_refs):
    m = [_scores_pass(q_ref[:, g * HEAD_DIM:(g + 1) * HEAD_DIM], k_refs, 0, HEAD_DIM, s_refs[g])
         for g in range(GQA_GROUP)]
    for g in range(GQA_GROUP):
        o = _values_pass(s_refs[g], m[g], v_refs, 0, HEAD_DIM)
        o_ref[:, g * HEAD_DIM:(g + 1) * HEAD_DIM] = o.astype(o_ref.dtype)


def _attn_kernel(q_ref, *rest, kind, n_src, n_chains, sub, lam_init):
    k_refs = rest[:n_src]
    v_refs = rest[n_src:2 * n_src]
    params = rest[2 * n_src:-n_chains - 1]
    o_ref = rest[-n_chains - 1]
    s_refs = rest[-n_chains:]
    if kind == "diff":
        _diff_attn_body(q_ref, k_refs, v_refs, params[0], params[1], o_ref, s_refs, sub, lam_init)
    else:
        _gqa_attn_body(q_ref, k_refs, v_refs, o_ref, s_refs)


def _attention_call(dm, q, k, v, *, kind, latent, out_rows, params, lam_init, prev=None):
    sub = dm.tq
    n_sub = DIFF_SUBTILES if (kind == "diff" and latent and dm.seq % (DIFF_SUBTILES * sub) == 0) else 1
    tq = sub * n_sub
    wq = 2 * HEAD_DIM if kind == "diff" else GQA_GROUP * HEAD_DIM
    wk = 2 * HEAD_DIM if kind == "diff" else HEAD_DIM
    n_heads = k.shape[1] // wk
    lat_per_ctx = dm.m_lat // dm.ctx
    lat_spec = pl.BlockSpec((dm.seq, wk), lambda b, h, qi: (b, h))
    ctx_spec = pl.BlockSpec((dm.ctx, wk), lambda b, h, qi: (lat_per_ctx + b, h))
    if latent:
        n_q, row0, srcs, n_keys = dm.seq // tq, 0, [lat_spec, ctx_spec], dm.seq + dm.ctx
    else:
        n_q, row0, srcs, n_keys = dm.ctx // tq, dm.m_lat // tq, [ctx_spec], dm.ctx
    q_spec = pl.BlockSpec((tq, wq), lambda b, h, qi: (row0 + b * n_q + qi, h))
    in_specs = [q_spec] + srcs + srcs + [pl.BlockSpec(p.shape, lambda b, h, qi: (0, 0)) for p in params]
    args = [q] + [k] * len(srcs) + [v] * len(srcs) + list(params)
    aliases = {}
    if prev is not None:
        aliases = {len(args): 0}
        in_specs.append(pl.BlockSpec(memory_space=pl.ANY))
        args.append(prev)
    n_chains = (2 if kind == "diff" else GQA_GROUP) * n_sub
    return pl.pallas_call(
        functools.partial(_attn_kernel, kind=kind, n_src=len(srcs), n_chains=n_chains, sub=sub,
                          lam_init=lam_init),
        out_shape=jax.ShapeDtypeStruct((out_rows, q.shape[1]), BF16),
        grid=(dm.b, n_heads, n_q),
        in_specs=in_specs,
        out_specs=q_spec,
        scratch_shapes=[pltpu.VMEM((sub, n_keys), F32) for _ in range(n_chains)],
        input_output_aliases=aliases,
        compiler_params=_cparams(3),
        name=kind + ("_attention" if latent else "_attention_ctx"),
    )(*args)


def _attention(dm, q, k, v, *, kind, ctx_out, params=(), lam_init=0.0):
    out_rows = dm.m_all if ctx_out else dm.m_lat
    common = dict(kind=kind, out_rows=out_rows, params=params, lam_init=lam_init)
    y = _attention_call(dm, q, k, v, latent=True, **common)
    if ctx_out:
        y = _attention_call(dm, q, k, v, latent=False, prev=y, **common)
    return y


def _dft_tables(n, norm):
    n0 = 1
    while n0 * n0 * 4 <= n and n % (n0 * 2) == 0:
        n0 *= 2
    n1 = n // n0
    k = jnp.arange(n, dtype=jnp.int32)[:, None]

    def cs(m):
        ang = ((k * m) % n).astype(F32) * (2.0 * math.pi / n)
        return jnp.cos(ang), jnp.sin(ang)

    c1, s1 = cs(jnp.arange(n1, dtype=jnp.int32)[None, :] * n0)
    c0, s0 = cs(jnp.arange(n0, dtype=jnp.int32)[None, :])
    c1, s1, c0, s0 = c1[:, :, None], s1[:, :, None], c0[:, None, :], s0[:, None, :]
    cos = (c1 * c0 - s1 * s0).reshape(n, n) * norm
    sin = (s1 * c0 + c1 * s0).reshape(n, n) * norm
    return cos, sin


def _dft_rows_kernel(c_ref, s_ref, yc_ref, ys_ref, *rest):
    o_ref = rest[-1]
    o_ref[...] = (jnp.dot(c_ref[...], yc_ref[...], preferred_element_type=F32)
                  + jnp.dot(s_ref[...], ys_ref[...], preferred_element_type=F32)).astype(o_ref.dtype)


def _dft_rows(dm, y, n, row0, out_rows, prev=None):
    d = dm.d
    dg = d // FOURIER_GROUPS
    cos, sin = _dft_tables(n, n ** -0.5)
    cos, nsin = cos.astype(BF16), (-sin).astype(BF16)
    tmf = min(ROW_TILE, n)
    tn = min(512, dg)
    lb = dg // tn
    rb0 = row0 // n

    def y_map(off):
        return lambda b, j, i: (rb0 + b, (j // lb) * 2 * lb + off + j % lb)

    in_specs = [pl.BlockSpec((tmf, n), lambda b, j, i: (i, 0)),
                pl.BlockSpec((tmf, n), lambda b, j, i: (i, 0)),
                pl.BlockSpec((n, tn), y_map(0)),
                pl.BlockSpec((n, tn), y_map(lb))]
    args = [cos, nsin, y, y]
    aliases = {}
    if prev is not None:
        in_specs.append(pl.BlockSpec(memory_space=pl.ANY))
        args.append(prev)
        aliases = {4: 0}
    tiles = n // tmf
    return pl.pallas_call(
        _dft_rows_kernel,
        out_shape=jax.ShapeDtypeStruct((out_rows, d), BF16),
        grid=(dm.b, d // tn, tiles),
        in_specs=in_specs,
        out_specs=pl.BlockSpec((tmf, tn), lambda b, j, i: ((rb0 + b) * tiles + i, j)),
        input_output_aliases=aliases,
        compiler_params=_cparams(3),
        name="dft_positions",
    )(*args)


def _fourier_mix(dm, h, ctx_out):
    d = dm.d
    dg = d // FOURIER_GROUPS
    rows = h.shape[0]
    cos_c, sin_c = _dft_tables(dg, dg ** -0.5)
    cs = jnp.concatenate([cos_c, sin_c], axis=1).astype(BF16)[None]
    y = _matmul(dm, h, cs, rows=rows, mode="plain", out_dtype=BF16, n_e=FOURIER_GROUPS,
                a_groups=True, tn=min(2 * dg, 1024))
    out_rows = dm.m_all if ctx_out else dm.m_lat
    f = _dft_rows(dm, y, dm.seq, 0, out_rows)
    if ctx_out:
        f = _dft_rows(dm, y, dm.ctx, dm.m_lat, out_rows, prev=f)
    return f


def _rope_tables(dm):
    rows = dm.seq // GRID_W
    r, col = jnp.meshgrid(jnp.arange(rows), jnp.arange(GRID_W), indexing="ij")
    pos = jnp.stack([r.reshape(-1), col.reshape(-1)], axis=-1).astype(F32)
    inv_freq = 1.0 / (ROPE_THETA ** (jnp.arange(ROPE_FREQS, dtype=F32) / ROPE_FREQS))
    ang = pos[:, :, None] * inv_freq
    cos, sin = jnp.cos(ang), jnp.sin(ang)
    zero = jnp.zeros_like(sin)
    cos_t = jnp.stack([cos, cos], axis=2).reshape(dm.seq, HEAD_DIM)
    sina_t = jnp.stack([-sin, zero], axis=2).reshape(dm.seq, HEAD_DIM)
    sinb_t = jnp.stack([zero, sin], axis=2).reshape(dm.seq, HEAD_DIM)

    def full(t, fill):
        return jnp.concatenate([jnp.tile(t, (dm.b, 1)), jnp.full((dm.m_ctx, HEAD_DIM), fill, F32)], axis=0)

    return full(cos_t, 1.0), full(sina_t, 0.0), full(sinb_t, 0.0)


def kernel(x, c, ctx, c_ctx, w_mod, b_mod, norm_g, diff_w_in, diff_w_out, diff_lambda, diff_subln_g,
           fourier_w_out, gqa_w_in, gqa_w_out, gqa_qk_g, ffn_w_in, ffn_w_out, moe_router, moe_w_in,
           moe_w_out, final_g):
    dm = _Dims(x, ctx)
    b, d = dm.b, dm.d
    depth = w_mod.shape[0]
    q_scale = HEAD_DIM ** -0.5 * LOG2E

    cc = jnp.concatenate([c, c_ctx[None, :], jnp.zeros((8 - b - 1, d), F32)], axis=0)
    mods = _modulations(cc, w_mod, b_mod)[:, :b + 1].reshape(depth, b + 1, N_MOD, d)
    tabs = _rope_tables(dm)
    xs = jnp.concatenate([x.reshape(dm.m_lat, d), ctx.reshape(dm.m_ctx, d)], axis=0)

    n_moe, n_e, _, two_f = moe_w_in.shape
    moe_w_in = moe_w_in.reshape(n_moe * n_e, d, two_f)
    moe_w_out = moe_w_out.reshape(n_moe, n_e * (two_f // 2), d).astype(BF16)

    for i in range(depth):
        last = i == depth - 1
        mod = mods[i]
        rows_in = xs.shape[0]
        rows_out = dm.m_lat if last else dm.m_all
        h = _norm_mod(dm, xs, norm_g[i, 0], mod, 0, rows_in)
        kind, j = i % N_MIXERS, i // N_MIXERS
        if kind == 0:
            qk = diff_w_in.shape[2] // 3
            q = _proj_rope(dm, h, diff_w_in, j, 0, qk, rows_out, tabs, scale=q_scale)
            k = _proj_rope(dm, h, diff_w_in, j, qk, qk, rows_in, tabs)
            v = _proj_plain(dm, h, diff_w_in, j, 2 * qk, qk, rows_in)
            lam_init = 0.8 - 0.6 * math.exp(-0.3 * i)
            y = _attention(dm, q, k, v, kind="diff", ctx_out=not last,
                           params=(diff_lambda[j], diff_subln_g[j].reshape(1, 2 * HEAD_DIM)),
                           lam_init=lam_init)
            w_out = diff_w_out
        elif kind == 1:
            y = _fourier_mix(dm, h, not last)
            w_out = fourier_w_out
        else:
            kvd = (gqa_w_in.shape[2] - d) // 2
            q = _proj_rope(dm, h, gqa_w_in, j, 0, d, rows_out, tabs, scale=q_scale, gain=gqa_qk_g[j, 0])
            k = _proj_rope(dm, h, gqa_w_in, j, d, kvd, rows_in, tabs, gain=gqa_qk_g[j, 1])
            v = _proj_plain(dm, h, gqa_w_in, j, d + kvd, kvd, rows_in)
            y = _attention(dm, q, k, v, kind="gqa", ctx_out=not last)
            w_out = gqa_w_out
        xs = _proj_resid(dm, y, w_out, j, xs, mod, 2, rows_out)

        f = i // 2
        if i % 2 == 0:
            z = _norm_mod(dm, xs, norm_g[i, 1], mod, 3, rows_out)
            hdn = _matmul(dm, z, ffn_w_in, rows=rows_out, mode="swiglu", out_dtype=BF16, e0=f,
                          swiglu_half=ffn_w_in.shape[2] // 2)
            xs = _proj_resid(dm, hdn, ffn_w_out, f, xs, mod, 5, rows_out)
        else:
            z, gates = _norm_mod(dm, xs, norm_g[i, 1], mod, 3, rows_out, w_router=moe_router[f])
            gate_spec = pl.BlockSpec((dm.tm, n_e), lambda jj, ii: (ii, 0))
            hdn = _matmul(dm, z, moe_w_in, rows=rows_out, mode="swiglu", out_dtype=BF16,
                          e0=f * n_e, n_e=n_e, swiglu_half=two_f // 2,
                          extra=[gates], extra_specs=[gate_spec])
            xs = _proj_resid(dm, hdn, moe_w_out, f, xs, mod, 5, rows_out)

    return _final_norm(dm, xs, final_g).reshape(b, dm.seq, d)
```

```python
import functools
import math

import jax
import jax.numpy as jnp
from jax import lax
from jax.experimental import pallas as pl
from jax.experimental.pallas import tpu as pltpu

HEAD_DIM = 128
GRID_W = 64
ROPE_THETA = 10000.0
ROPE_FREQS = HEAD_DIM // 4
NORM_EPS = 1e-6
N_MOD = 6
N_MIXERS = 3
FOURIER_GROUPS = 4
GQA_GROUP = 4

LANES = 128
VMEM_LIMIT_BYTES = 56 * 2**20
ROW_TILE = 512
NORM_ROW_TILE = 256
MOE_TILE = 256
Q_TILE = 256
DIFF_SUBTILES = 2
KEY_CHUNK = 512
LOG2E = math.log2(math.e)
W_TILE_BYTES = 12 * 2**20

F32 = jnp.float32
BF16 = jnp.bfloat16


def _cparams(n_axes):
    return pltpu.CompilerParams(dimension_semantics=("arbitrary",) * n_axes,
                                vmem_limit_bytes=VMEM_LIMIT_BYTES)


def _silu(v):
    return v / (1.0 + jnp.exp(-v))


class _Dims:
    def __init__(self, x, ctx):
        self.b, self.seq, self.d = x.shape
        self.ctx = ctx.shape[1]
        self.m_lat = self.b * self.seq
        self.m_ctx = self.b * self.ctx
        self.m_all = self.m_lat + self.m_ctx
        g = math.gcd(self.seq, self.m_ctx)
        self.tm = min(ROW_TILE, g)
        self.tr = min(NORM_ROW_TILE, g)
        self.tq = min(Q_TILE, self.ctx)
        assert self.seq % self.tq == 0 and self.ctx % self.tq == 0

    def group(self, i, tile):
        r = i * tile
        return jnp.where(r < self.m_lat, r // self.seq, self.b)


def _mod_kernel(c_ref, w_ref, b_ref, o_ref):
    s = _silu(c_ref[...]).astype(BF16)
    w = w_ref[...].astype(BF16)
    o_ref[...] = jnp.dot(s, w, preferred_element_type=F32) + b_ref[...]


def _modulations(cc, w_mod, b_mod):
    depth, d, n = w_mod.shape
    tn = 512 if n % 512 == 0 else n
    rows = cc.shape[0]
    return pl.pallas_call(
        _mod_kernel,
        out_shape=jax.ShapeDtypeStruct((depth, rows, n), F32),
        grid=(depth, n // tn),
        in_specs=[pl.BlockSpec((rows, d), lambda l, j: (0, 0)),
                  pl.BlockSpec((None, d, tn), lambda l, j: (l, 0, j)),
                  pl.BlockSpec((None, 1, tn), lambda l, j: (l, 0, j))],
        out_specs=pl.BlockSpec((None, rows, tn), lambda l, j: (l, 0, j)),
        compiler_params=_cparams(2),
        name="adaln_modulations",
    )(cc, w_mod, b_mod.reshape(depth, 1, n))


def _norm_mod_value(x_ref, g_ref, mod_ref, ci):
    x = x_ref[...]
    y = x * lax.rsqrt(jnp.mean(x * x, axis=-1, keepdims=True) + NORM_EPS) * g_ref[...]
    shift = mod_ref[ci:ci + 1, :]
    scale = mod_ref[ci + 1:ci + 2, :]
    return y * (1.0 + scale) + shift


def _norm_mod_kernel(x_ref, g_ref, mod_ref, o_ref, *, ci):
    o_ref[...] = _norm_mod_value(x_ref, g_ref, mod_ref, ci).astype(BF16)


def _split_bf16(v):
    hi = v.astype(BF16)
    lo = (v - hi.astype(F32)).astype(BF16)
    return hi, lo


ROUTE_COLS = 8


def _norm_mod_router_kernel(x_ref, g_ref, mod_ref, wr_ref, o_ref, route_ref, counts_ref, run_ref, *, ci, n_e):
    z = _norm_mod_value(x_ref, g_ref, mod_ref, ci)
    o_ref[...] = z
    z_hi, z_lo = _split_bf16(z)
    w_hi, w_lo = _split_bf16(wr_ref[...])
    logits = (jnp.dot(z_hi, w_hi, preferred_element_type=F32)
              + jnp.dot(z_lo, w_hi, preferred_element_type=F32)
              + jnp.dot(z_hi, w_lo, preferred_element_type=F32))
    idx = lax.broadcasted_iota(jnp.int32, logits.shape, 1).astype(F32)
    logits = jnp.where(idx < n_e, logits, -jnp.inf)
    m1 = jnp.max(logits, axis=-1, keepdims=True)
    i1 = jnp.min(jnp.where(logits == m1, idx, float(LANES)), axis=-1, keepdims=True)
    rest = jnp.where(idx == i1, -jnp.inf, logits)
    m2 = jnp.max(rest, axis=-1, keepdims=True)
    i2 = jnp.min(jnp.where(rest == m2, idx, float(LANES)), axis=-1, keepdims=True)
    e2 = jnp.exp(m2 - m1)
    w1 = 1.0 / (1.0 + e2)
    w2 = e2 * w1

    @pl.when(pl.program_id(0) == 0)
    def _():
        run_ref[...] = jnp.zeros_like(run_ref)

    sel1, sel2 = idx == i1, idx == i2
    picked = jnp.where(sel1 | sel2, 1.0, 0.0)
    rows = picked.shape[0]
    earlier = (lax.broadcasted_iota(jnp.int32, (rows, rows), 0)
               > lax.broadcasted_iota(jnp.int32, (rows, rows), 1))
    before = jnp.dot(jnp.where(earlier, 1.0, 0.0).astype(BF16), picked.astype(BF16),
                     preferred_element_type=F32) + run_ref[...]
    rank1 = jnp.sum(jnp.where(sel1, before, 0.0), axis=-1, keepdims=True)
    rank2 = jnp.sum(jnp.where(sel2, before, 0.0), axis=-1, keepdims=True)
    run_ref[...] = run_ref[...] + jnp.sum(picked, axis=0, keepdims=True)
    counts_ref[...] = run_ref[...]

    route = jnp.zeros_like(logits)
    for col, val in enumerate((i1, i2, rank1, rank2, w1, w2)):
        route = jnp.where(idx == col, val, route)
    route_ref[...] = route[:, :ROUTE_COLS]


def _norm_mod(dm, x, g, mod, ci, rows, w_router=None):
    d = x.shape[1]
    tr = dm.tr
    in_specs = [pl.BlockSpec((tr, d), lambda i: (i, 0)),
                pl.BlockSpec((1, d), lambda i: (0, 0)),
                pl.BlockSpec((None, N_MOD, d), lambda i: (dm.group(i, tr), 0, 0))]
    h_spec = pl.BlockSpec((tr, d), lambda i: (i, 0))
    h_shape = jax.ShapeDtypeStruct((rows, d), BF16)
    if w_router is None:
        return pl.pallas_call(
            functools.partial(_norm_mod_kernel, ci=ci),
            out_shape=h_shape, grid=(rows // tr,), in_specs=in_specs, out_specs=h_spec,
            compiler_params=_cparams(1), name="norm_modulate",
        )(x, g.reshape(1, d), mod)
    n_e = w_router.shape[1]
    assert n_e <= LANES
    w_router = jnp.pad(w_router, ((0, 0), (0, LANES - n_e)))
    return pl.pallas_call(
        functools.partial(_norm_mod_router_kernel, ci=ci, n_e=n_e),
        out_shape=(jax.ShapeDtypeStruct((rows, d), F32),
                   jax.ShapeDtypeStruct((rows, ROUTE_COLS), F32),
                   jax.ShapeDtypeStruct((1, LANES), F32)),
        grid=(rows // tr,),
        in_specs=in_specs + [pl.BlockSpec((d, LANES), lambda i: (0, 0))],
        out_specs=(h_spec, pl.BlockSpec((tr, ROUTE_COLS), lambda i: (i, 0)),
                   pl.BlockSpec((1, LANES), lambda i: (0, 0))),
        scratch_shapes=[pltpu.VMEM((1, LANES), F32)],
        compiler_params=_cparams(1), name="norm_modulate_router",
    )(x, g.reshape(1, d), mod, w_router)


def _final_norm_kernel(x_ref, g_ref, o_ref):
    x = x_ref[...]
    o_ref[...] = x * lax.rsqrt(jnp.mean(x * x, axis=-1, keepdims=True) + NORM_EPS) * g_ref[...]


def _final_norm(dm, x, g):
    rows, d = x.shape
    tr = dm.tr
    return pl.pallas_call(
        _final_norm_kernel,
        out_shape=jax.ShapeDtypeStruct((rows, d), F32),
        grid=(rows // tr,),
        in_specs=[pl.BlockSpec((tr, d), lambda i: (i, 0)), pl.BlockSpec((1, d), lambda i: (0, 0))],
        out_specs=pl.BlockSpec((tr, d), lambda i: (i, 0)),
        compiler_params=_cparams(1), name="final_norm",
    )(x, g.reshape(1, d))


def _rope_epilogue(acc, cos_ref, sina_ref, sinb_ref, g_ref, o_ref, scale):
    cos, sina, sinb = cos_ref[...], sina_ref[...], sinb_ref[...]
    for c in range(acc.shape[1] // HEAD_DIM):
        xh = acc[:, c * HEAD_DIM:(c + 1) * HEAD_DIM]
        if g_ref is not None:
            xh = xh * lax.rsqrt(jnp.mean(xh * xh, axis=-1, keepdims=True) + NORM_EPS) * g_ref[...]
        fwd = pltpu.roll(xh, HEAD_DIM - ROPE_FREQS, 1)
        bwd = pltpu.roll(xh, ROPE_FREQS, 1)
        r = xh * cos + fwd * sina + bwd * sinb
        if scale != 1.0:
            r = r * scale
        o_ref[:, c * HEAD_DIM:(c + 1) * HEAD_DIM] = r.astype(o_ref.dtype)


def _mm_kernel(*refs, mode, n_a, lat_tiles, n_w, cast, n_extra, gate_idx, scale):
    a_refs = refs[:n_a]
    w_refs = refs[n_a:n_a + n_w]
    extra = refs[n_a + n_w:n_a + n_w + n_extra]
    o_ref = refs[n_a + n_w + n_extra]
    wb_refs = refs[n_a + n_w + n_extra + 1:]

    if cast:
        @pl.when(pl.program_id(1) == 0)
        def _():
            for w_ref, wb_ref in zip(w_refs, wb_refs):
                wb_ref[...] = w_ref[...].astype(BF16)
        w_srcs = wb_refs
    else:
        w_srcs = w_refs

    if n_a == 1:
        _mm_tile(a_refs[0], w_srcs, extra, o_ref, mode, gate_idx, scale)
        return

    i = pl.program_id(1)

    @pl.when(i < lat_tiles)
    def _():
        _mm_tile(a_refs[0], w_srcs, extra, o_ref, mode, gate_idx, scale)

    @pl.when(i >= lat_tiles)
    def _():
        _mm_tile(a_refs[1], w_srcs, extra, o_ref, mode, gate_idx, scale)


def _mm_tile(a_ref, w_srcs, extra, o_ref, mode, gate_idx, scale):
    a = a_ref[...]
    accs = [jnp.dot(a, w[...], preferred_element_type=F32) for w in w_srcs]
    n_extra = len(extra)

    if mode == "plain":
        o_ref[...] = accs[0].astype(o_ref.dtype)
    elif mode == "rope":
        g_ref = extra[3] if n_extra == 4 else None
        _rope_epilogue(accs[0], extra[0], extra[1], extra[2], g_ref, o_ref, scale)
    elif mode == "resid":
        x_ref, mod_ref = extra
        gate = mod_ref[gate_idx:gate_idx + 1, :]
        o_ref[...] = x_ref[...] + gate * accs[0]
    elif mode == "swiglu":
        o_ref[...] = (_silu(accs[0]) * accs[1]).astype(o_ref.dtype)
    else:
        raise ValueError(mode)


def _pick_tn(k, n, itemsize, n_w):
    best = None
    for tn in range(LANES, n + 1, LANES):
        if n % tn == 0 and k * tn * itemsize * n_w <= W_TILE_BYTES:
            best = tn
    return best if best is not None else (LANES if n % LANES == 0 else n)


def _matmul(dm, a, w, *, rows, mode, out_dtype, e0=0, n_e=1, col0=0, n_cols=None, tn=None,
            extra=(), extra_specs=(), gate_idx=0, scale=1.0, swiglu_half=None, a_groups=False):
    a_parts = list(a) if isinstance(a, (list, tuple)) else [a]
    _, k, n_w_cols = w.shape
    assert all(p.shape[1] == (n_e * k if a_groups else k) for p in a_parts)
    cast = w.dtype != BF16
    n_w = 2 if mode == "swiglu" else 1
    if n_cols is None:
        n_cols = swiglu_half if mode == "swiglu" else n_w_cols - col0
    if tn is None:
        tn = _pick_tn(k, n_cols, w.dtype.itemsize, n_w)
    tm = dm.tm
    assert n_cols % tn == 0 and col0 % tn == 0 and rows % tm == 0
    bpe = n_cols // tn
    blk0 = col0 // tn

    def w_map(off):
        if a_groups:
            return lambda j, i: (e0, 0, blk0 + off + j % bpe)
        return lambda j, i: (e0 + j // bpe, 0, blk0 + off + j % bpe)

    lat_tiles = a_parts[0].shape[0] // tm
    if len(a_parts) == 1:
        a_maps = [(lambda j, i: (i, j // bpe)) if a_groups else (lambda j, i: (i, 0))]
    else:
        assert not a_groups and a_parts[0].shape[0] % tm == 0
        assert rows == a_parts[0].shape[0] + a_parts[1].shape[0]
        a_maps = [lambda j, i: (jnp.minimum(i, lat_tiles - 1), 0),
                  lambda j, i: (jnp.maximum(i - lat_tiles, 0), 0)]
    in_specs = [pl.BlockSpec((tm, k), m) for m in a_maps] + [pl.BlockSpec((None, k, tn), w_map(0))]
    args = a_parts + [w]
    if n_w == 2:
        assert swiglu_half % tn == 0
        in_specs.append(pl.BlockSpec((None, k, tn), w_map(swiglu_half // tn)))
        args.append(w)
    in_specs += list(extra_specs)
    args += list(extra)
    scratch = [pltpu.VMEM((k, tn), BF16) for _ in range(n_w)] if cast else []
    kern = functools.partial(_mm_kernel, mode=mode, n_a=len(a_parts), lat_tiles=lat_tiles, n_w=n_w,
                             cast=cast, n_extra=len(extra), gate_idx=gate_idx, scale=scale)
    return pl.pallas_call(
        kern,
        out_shape=jax.ShapeDtypeStruct((rows, n_e * n_cols), out_dtype),
        grid=(n_e * bpe, rows // tm),
        in_specs=in_specs,
        out_specs=pl.BlockSpec((tm, tn), lambda j, i: (i, j)),
        scratch_shapes=scratch,
        compiler_params=_cparams(2),
        name="matmul_" + mode,
    )(*args)


def _proj_rope(dm, h, w, layer, col0, n_cols, rows, tabs, scale=1.0, gain=None):
    tm = dm.tm
    tab_spec = pl.BlockSpec((tm, HEAD_DIM), lambda j, i: (i, 0))
    extra, specs = list(tabs), [tab_spec] * 3
    if gain is not None:
        extra.append(gain.reshape(1, HEAD_DIM))
        specs.append(pl.BlockSpec((1, HEAD_DIM), lambda j, i: (0, 0)))
    return _matmul(dm, h, w, rows=rows, mode="rope", out_dtype=BF16, e0=layer, col0=col0,
                   n_cols=n_cols, extra=extra, extra_specs=specs, scale=scale)


def _proj_plain(dm, h, w, layer, col0, n_cols, rows):
    return _matmul(dm, h, w, rows=rows, mode="plain", out_dtype=BF16, e0=layer, col0=col0,
                   n_cols=n_cols)


def _proj_resid(dm, a, w, layer, x, mod, gate_idx, rows):
    tm = dm.tm
    _, k, n = w.shape
    tn = _pick_tn(k, n, w.dtype.itemsize, 1)
    specs = [pl.BlockSpec((tm, tn), lambda j, i: (i, j)),
             pl.BlockSpec((None, N_MOD, tn), lambda j, i: (dm.group(i, tm), 0, j))]
    return _matmul(dm, a, w, rows=rows, mode="resid", out_dtype=F32, e0=layer, tn=tn,
                   extra=[x, mod], extra_specs=specs, gate_idx=gate_idx)


def _route_tables(route, counts, n_e, rows):
    tg = MOE_TILE
    expert = route[:, 0:2].astype(jnp.int32)
    rank = route[:, 2:4].astype(jnp.int32)
    cnt = counts[0, :n_e].astype(jnp.int32)
    padded = (cnt + tg - 1) // tg * tg
    ends = jnp.cumsum(padded)
    slots = (ends - padded)[expert] + rank
    assert (2 * rows) % tg == 0
    n_tiles = 2 * rows // tg + n_e
    tok = jnp.broadcast_to(jnp.arange(rows, dtype=jnp.int32)[:, None], (rows, 2))
    src = jnp.zeros((n_tiles * tg,), jnp.int32).at[slots.reshape(-1)].set(tok.reshape(-1))
    tile_row0 = jnp.arange(n_tiles, dtype=jnp.int32)[:, None] * tg
    tile_expert = jnp.minimum(jnp.sum(tile_row0 >= ends[None, :], axis=1), n_e - 1).astype(jnp.int32)
    return slots, src, tile_expert, n_tiles


def _rows_copy(src_hbm, dst, sem, n):
    return pltpu.make_async_copy(src_hbm.at[pl.ds(0, n), :], dst, sem)


def _dispatch_kernel(src_ref, z_hbm, o_hbm, sem):
    tg = src_ref.shape[-1]
    base = pl.program_id(0) * tg

    def issue(r, carry):
        pltpu.make_async_copy(z_hbm.at[pl.ds(src_ref[0, r], 1), :],
                              o_hbm.at[pl.ds(base + r, 1), :], sem).start()
        return carry

    lax.fori_loop(0, tg, issue, 0)
    _rows_copy(z_hbm, o_hbm.at[pl.ds(base, tg), :], sem, tg).wait()


def _dispatch(z, src, n_tiles):
    tg = MOE_TILE
    return pl.pallas_call(
        _dispatch_kernel,
        out_shape=jax.ShapeDtypeStruct((n_tiles * tg, z.shape[1]), z.dtype),
        grid=(n_tiles,),
        in_specs=[pl.BlockSpec((None, 1, tg), lambda g: (g, 0, 0), memory_space=pltpu.SMEM),
                  pl.BlockSpec(memory_space=pl.ANY)],
        out_specs=pl.BlockSpec(memory_space=pl.ANY),
        scratch_shapes=[pltpu.SemaphoreType.DMA(())],
        compiler_params=_cparams(1), name="moe_dispatch",
    )(src.reshape(n_tiles, 1, tg), z)


def _expert_changed(te_ref):
    g = pl.program_id(1)
    return (g == 0) | (te_ref[g] != te_ref[jnp.maximum(g - 1, 0)])


def _moe_up_kernel(te_ref, a_ref, wg_ref, wu_ref, o_ref, wgb_ref, wub_ref):
    @pl.when(_expert_changed(te_ref))
    def _():
        wgb_ref[...] = wg_ref[...].astype(BF16)
        wub_ref[...] = wu_ref[...].astype(BF16)

    a = a_ref[...].astype(BF16)
    hg = jnp.dot(a, wgb_ref[...], preferred_element_type=F32)
    hu = jnp.dot(a, wub_ref[...], preferred_element_type=F32)
    o_ref[...] = (_silu(hg) * hu).astype(o_ref.dtype)


def _moe_down_kernel(te_ref, a_ref, w_ref, o_ref, wb_ref):
    @pl.when(_expert_changed(te_ref))
    def _():
        wb_ref[...] = w_ref[...].astype(BF16)

    o_ref[...] = jnp.dot(a_ref[...], wb_ref[...], preferred_element_type=F32)


def _moe_up(zg, w_in, e0, tile_expert, f):
    tg = MOE_TILE
    s, d = zg.shape
    tn = _pick_tn(d, f, 4, 2)
    nb = f // tn
    grid_spec = pltpu.PrefetchScalarGridSpec(
        num_scalar_prefetch=1, grid=(nb, s // tg),
        in_specs=[pl.BlockSpec((tg, d), lambda j, g, te: (g, 0)),
                  pl.BlockSpec((None, d, tn), lambda j, g, te: (e0 + te[g], 0, j)),
                  pl.BlockSpec((None, d, tn), lambda j, g, te: (e0 + te[g], 0, nb + j))],
        out_specs=pl.BlockSpec((tg, tn), lambda j, g, te: (g, j)),
        scratch_shapes=[pltpu.VMEM((d, tn), BF16), pltpu.VMEM((d, tn), BF16)])
    return pl.pallas_call(
        _moe_up_kernel, out_shape=jax.ShapeDtypeStruct((s, f), BF16), grid_spec=grid_spec,
        compiler_params=_cparams(2), name="moe_up",
    )(tile_expert, zg, w_in, w_in)


def _moe_down(hdn, w_out, e0, tile_expert):
    tg = MOE_TILE
    s, f = hdn.shape
    d = w_out.shape[2]
    tn = _pick_tn(f, d, 4, 1)
    grid_spec = pltpu.PrefetchScalarGridSpec(
        num_scalar_prefetch=1, grid=(d // tn, s // tg),
        in_specs=[pl.BlockSpec((tg, f), lambda j, g, te: (g, 0)),
                  pl.BlockSpec((None, f, tn), lambda j, g, te: (e0 + te[g], 0, j))],
        out_specs=pl.BlockSpec((tg, tn), lambda j, g, te: (g, j)),
        scratch_shapes=[pltpu.VMEM((f, tn), BF16)])
    return pl.pallas_call(
        _moe_down_kernel, out_shape=jax.ShapeDtypeStruct((s, d), F32), grid_spec=grid_spec,
        compiler_params=_cparams(2), name="moe_down",
    )(tile_expert, hdn, w_out)


def _combine_kernel(slot_ref, y_hbm, x_ref, route_ref, mod_ref, o_ref, y0_ref, y1_ref, sem, *, gate_idx):
    tr = x_ref.shape[0]

    def issue(r, carry):
        pltpu.make_async_copy(y_hbm.at[pl.ds(slot_ref[0, 2 * r], 1), :],
                              y0_ref.at[pl.ds(r, 1), :], sem.at[0]).start()
        pltpu.make_async_copy(y_hbm.at[pl.ds(slot_ref[0, 2 * r + 1], 1), :],
                              y1_ref.at[pl.ds(r, 1), :], sem.at[1]).start()
        return carry

    lax.fori_loop(0, tr, issue, 0)
    _rows_copy(y_hbm, y0_ref, sem.at[0], tr).wait()
    _rows_copy(y_hbm, y1_ref, sem.at[1], tr).wait()
    route = route_ref[...]
    gate = mod_ref[gate_idx:gate_idx + 1, :]
    o_ref[...] = x_ref[...] + gate * (route[:, 4:5] * y0_ref[...] + route[:, 5:6] * y1_ref[...])


def _moe_combine(dm, yg, slots, x, route, mod, gate_idx, rows):
    tr = dm.tr
    d = x.shape[1]
    return pl.pallas_call(
        functools.partial(_combine_kernel, gate_idx=gate_idx),
        out_shape=jax.ShapeDtypeStruct((rows, d), F32),
        grid=(rows // tr,),
        in_specs=[pl.BlockSpec((None, 1, 2 * tr), lambda i: (i, 0, 0), memory_space=pltpu.SMEM),
                  pl.BlockSpec(memory_space=pl.ANY),
                  pl.BlockSpec((tr, d), lambda i: (i, 0)),
                  pl.BlockSpec((tr, ROUTE_COLS), lambda i: (i, 0)),
                  pl.BlockSpec((None, N_MOD, d), lambda i: (dm.group(i, tr), 0, 0))],
        out_specs=pl.BlockSpec((tr, d), lambda i: (i, 0)),
        scratch_shapes=[pltpu.VMEM((tr, d), F32), pltpu.VMEM((tr, d), F32),
                        pltpu.SemaphoreType.DMA((2,))],
        compiler_params=_cparams(1), name="moe_combine",
    )(slots.reshape(rows // tr, 1, 2 * tr), yg, x, route, mod)


_NT = (((1,), (1,)), ((), ()))


def _key_chunks(refs):
    out, off = [], 0
    for r in refs:
        n = r.shape[0]
        ck = min(KEY_CHUNK, n)
        for r0 in range(0, n, ck):
            out.append((r, r0, ck, off))
            off += ck
    return out


def _lane_fold(acc, v, op):
    for t in range(v.shape[1] // LANES):
        piece = v[:, t * LANES:(t + 1) * LANES]
        acc = piece if acc is None else op(acc, piece)
    return acc


def _scores_pass(q, k_refs, lo, hi, s_ref):
    mx = None
    for r, r0, ck, off in _key_chunks(k_refs):
        s = lax.dot_general(q, r[r0:r0 + ck, lo:hi], _NT, preferred_element_type=F32)
        s_ref[:, off:off + ck] = s
        mx = _lane_fold(mx, s, jnp.maximum)
    return jnp.max(mx, axis=-1, keepdims=True)


def _values_pass(s_ref, m, v_refs, lo, hi):
    acc, ls = None, None
    for r, r0, ck, off in _key_chunks(v_refs):
        e = jnp.exp2(s_ref[:, off:off + ck] - m)
        ls = _lane_fold(ls, e, jnp.add)
        pv = jnp.dot(e.astype(BF16), r[r0:r0 + ck, lo:hi], preferred_element_type=F32)
        acc = pv if acc is None else acc + pv
    return acc * (1.0 / jnp.sum(ls, axis=-1, keepdims=True))


def _diff_attn_body(q_ref, k_refs, v_refs, lam_ref, g_ref, o_ref, s_refs, sub, lam_init):
    lv = lam_ref[...]
    lam = (jnp.exp(jnp.sum(lv[0:1] * lv[1:2], axis=-1, keepdims=True))
           - jnp.exp(jnp.sum(lv[2:3] * lv[3:4], axis=-1, keepdims=True)) + lam_init)
    n_sub = q_ref.shape[0] // sub
    chains = [(t, c) for t in range(n_sub) for c in range(2)]
    m = {}
    for t, c in chains:
        lo, hi = c * HEAD_DIM, (c + 1) * HEAD_DIM
        m[t, c] = _scores_pass(q_ref[t * sub:(t + 1) * sub, lo:hi], k_refs, lo, hi, s_refs[2 * t + c])
    outs = {ch: _values_pass(s_refs[2 * ch[0] + ch[1]], m[ch], v_refs, 0, 2 * HEAD_DIM) for ch in chains}
    for t in range(n_sub):
        o = outs[t, 0] - lam * outs[t, 1]
        o = o * lax.rsqrt(jnp.mean(o * o, axis=-1, keepdims=True) + NORM_EPS) * g_ref[...]
        o_ref[t * sub:(t + 1) * sub, :] = (o * (1.0 - lam_init)).astype(o_ref.dtype)


def _gqa_attn_body(q_ref, k_refs, v_refs, o_ref, s_refs):
    m = [_scores_pass(q_ref[:, g * HEAD_DIM:(g + 1) * HEAD_DIM], k_refs, 0, HEAD_DIM, s_refs[g])
         for g in range(GQA_GROUP)]
    for g in range(GQA_GROUP):
        o = _values_pass(s_refs[g], m[g], v_refs, 0, HEAD_DIM)
        o_ref[:, g * HEAD_DIM:(g + 1) * HEAD_DIM] = o.astype(o_ref.dtype)


def _attn_kernel(q_ref, *rest, kind, n_src, n_chains, sub, lam_init):
    k_refs = rest[:n_src]
    v_refs = rest[n_src:2 * n_src]
    params = rest[2 * n_src:-n_chains - 1]
    o_ref = rest[-n_chains - 1]
    s_refs = rest[-n_chains:]
    if kind == "diff":
        _diff_attn_body(q_ref, k_refs, v_refs, params[0], params[1], o_ref, s_refs, sub, lam_init)
    else:
        _gqa_attn_body(q_ref, k_refs, v_refs, o_ref, s_refs)


def _attention_call(dm, q, k, v, *, kind, latent, params, lam_init):
    sub = dm.tq
    n_sub = DIFF_SUBTILES if (kind == "diff" and latent and dm.seq % (DIFF_SUBTILES * sub) == 0) else 1
    tq = sub * n_sub
    wq = 2 * HEAD_DIM if kind == "diff" else GQA_GROUP * HEAD_DIM
    wk = 2 * HEAD_DIM if kind == "diff" else HEAD_DIM
    n_heads = k.shape[1] // wk
    lat_per_ctx = dm.m_lat // dm.ctx
    lat_spec = pl.BlockSpec((dm.seq, wk), lambda b, h, qi: (b, h))
    ctx_spec = pl.BlockSpec((dm.ctx, wk), lambda b, h, qi: (lat_per_ctx + b, h))
    if latent:
        n_q, row0, srcs, n_keys = dm.seq // tq, 0, [lat_spec, ctx_spec], dm.seq + dm.ctx
    else:
        n_q, row0, srcs, n_keys = dm.ctx // tq, dm.m_lat // tq, [ctx_spec], dm.ctx
    q_spec = pl.BlockSpec((tq, wq), lambda b, h, qi: (row0 + b * n_q + qi, h))
    in_specs = [q_spec] + srcs + srcs + [pl.BlockSpec(p.shape, lambda b, h, qi: (0, 0)) for p in params]
    args = [q] + [k] * len(srcs) + [v] * len(srcs) + list(params)
    n_chains = (2 if kind == "diff" else GQA_GROUP) * n_sub
    return pl.pallas_call(
        functools.partial(_attn_kernel, kind=kind, n_src=len(srcs), n_chains=n_chains, sub=sub,
                          lam_init=lam_init),
        out_shape=jax.ShapeDtypeStruct((dm.b * n_q * tq, q.shape[1]), BF16),
        grid=(dm.b, n_heads, n_q),
        in_specs=in_specs,
        out_specs=pl.BlockSpec((tq, wq), lambda b, h, qi: (b * n_q + qi, h)),
        scratch_shapes=[pltpu.VMEM((sub, n_keys), F32) for _ in range(n_chains)],
        compiler_params=_cparams(3),
        name=kind + ("_attention" if latent else "_attention_ctx"),
    )(*args)


def _attention(dm, q, k, v, *, kind, ctx_out, params=(), lam_init=0.0):
    common = dict(kind=kind, params=params, lam_init=lam_init)
    y = [_attention_call(dm, q, k, v, latent=True, **common)]
    if ctx_out:
        y.append(_attention_call(dm, q, k, v, latent=False, **common))
    return y


def _dft_tables(n, norm):
    n0 = 1
    while n0 * n0 * 4 <= n and n % (n0 * 2) == 0:
        n0 *= 2
    n1 = n // n0
    k = jnp.arange(n, dtype=jnp.int32)[:, None]

    def cs(m):
        ang = ((k * m) % n).astype(F32) * (2.0 * math.pi / n)
        return jnp.cos(ang), jnp.sin(ang)

    c1, s1 = cs(jnp.arange(n1, dtype=jnp.int32)[None, :] * n0)
    c0, s0 = cs(jnp.arange(n0, dtype=jnp.int32)[None, :])
    c1, s1, c0, s0 = c1[:, :, None], s1[:, :, None], c0[:, None, :], s0[:, None, :]
    cos = (c1 * c0 - s1 * s0).reshape(n, n) * norm
    sin = (s1 * c0 + c1 * s0).reshape(n, n) * norm
    return cos, sin


def _dft_rows_kernel(c_ref, s_ref, yc_ref, ys_ref, o_ref):
    o_ref[...] = (jnp.dot(c_ref[...], yc_ref[...], preferred_element_type=F32)
                  + jnp.dot(s_ref[...], ys_ref[...], preferred_element_type=F32)).astype(o_ref.dtype)


def _dft_rows(dm, y, n, row0):
    d = dm.d
    dg = d // FOURIER_GROUPS
    cos, sin = _dft_tables(n, n ** -0.5)
    cos, nsin = cos.astype(BF16), (-sin).astype(BF16)
    tmf = min(ROW_TILE, n)
    tn = min(512, dg)
    lb = dg // tn
    rb0 = row0 // n

    def y_map(off):
        return lambda b, j, i: (rb0 + b, (j // lb) * 2 * lb + off + j % lb)

    in_specs = [pl.BlockSpec((tmf, n), lambda b, j, i: (i, 0)),
                pl.BlockSpec((tmf, n), lambda b, j, i: (i, 0)),
                pl.BlockSpec((n, tn), y_map(0)),
                pl.BlockSpec((n, tn), y_map(lb))]
    tiles = n // tmf
    return pl.pallas_call(
        _dft_rows_kernel,
        out_shape=jax.ShapeDtypeStruct((dm.b * n, d), BF16),
        grid=(dm.b, d // tn, tiles),
        in_specs=in_specs,
        out_specs=pl.BlockSpec((tmf, tn), lambda b, j, i: (b * tiles + i, j)),
        compiler_params=_cparams(3),
        name="dft_positions",
    )(cos, nsin, y, y)


def _fourier_mix(dm, h, ctx_out):
    d = dm.d
    dg = d // FOURIER_GROUPS
    rows = h.shape[0]
    cos_c, sin_c = _dft_tables(dg, dg ** -0.5)
    cs = jnp.concatenate([cos_c, sin_c], axis=1).astype(BF16)[None]
    y = _matmul(dm, h, cs, rows=rows, mode="plain", out_dtype=BF16, n_e=FOURIER_GROUPS,
                a_groups=True, tn=min(2 * dg, 1024))
    f = [_dft_rows(dm, y, dm.seq, 0)]
    if ctx_out:
        f.append(_dft_rows(dm, y, dm.ctx, dm.m_lat))
    return f


def _rope_tables(dm):
    rows = dm.seq // GRID_W
    r, col = jnp.meshgrid(jnp.arange(rows), jnp.arange(GRID_W), indexing="ij")
    pos = jnp.stack([r.reshape(-1), col.reshape(-1)], axis=-1).astype(F32)
    inv_freq = 1.0 / (ROPE_THETA ** (jnp.arange(ROPE_FREQS, dtype=F32) / ROPE_FREQS))
    ang = pos[:, :, None] * inv_freq
    cos, sin = jnp.cos(ang), jnp.sin(ang)
    zero = jnp.zeros_like(sin)
    cos_t = jnp.stack([cos, cos], axis=2).reshape(dm.seq, HEAD_DIM)
    sina_t = jnp.stack([-sin, zero], axis=2).reshape(dm.seq, HEAD_DIM)
    sinb_t = jnp.stack([zero, sin], axis=2).reshape(dm.seq, HEAD_DIM)

    def full(t, fill):
        return jnp.concatenate([jnp.tile(t, (dm.b, 1)), jnp.full((dm.m_ctx, HEAD_DIM), fill, F32)], axis=0)

    return full(cos_t, 1.0), full(sina_t, 0.0), full(sinb_t, 0.0)


def kernel(x, c, ctx, c_ctx, w_mod, b_mod, norm_g, diff_w_in, diff_w_out, diff_lambda, diff_subln_g,
           fourier_w_out, gqa_w_in, gqa_w_out, gqa_qk_g, ffn_w_in, ffn_w_out, moe_router, moe_w_in,
           moe_w_out, final_g):
    dm = _Dims(x, ctx)
    b, d = dm.b, dm.d
    depth = w_mod.shape[0]
    q_scale = HEAD_DIM ** -0.5 * LOG2E

    cc = jnp.concatenate([c, c_ctx[None, :], jnp.zeros((8 - b - 1, d), F32)], axis=0)
    mods = _modulations(cc, w_mod, b_mod)[:, :b + 1].reshape(depth, b + 1, N_MOD, d)
    tabs = _rope_tables(dm)
    xs = jnp.concatenate([x.reshape(dm.m_lat, d), ctx.reshape(dm.m_ctx, d)], axis=0)

    n_moe, n_e, _, two_f = moe_w_in.shape
    moe_w_in = moe_w_in.reshape(n_moe * n_e, d, two_f)
    moe_w_out = moe_w_out.reshape(n_moe * n_e, two_f // 2, d)

    for i in range(depth):
        last = i == depth - 1
        mod = mods[i]
        rows_in = xs.shape[0]
        rows_out = dm.m_lat if last else dm.m_all
        h = _norm_mod(dm, xs, norm_g[i, 0], mod, 0, rows_in)
        kind, j = i % N_MIXERS, i // N_MIXERS
        if kind == 0:
            qk = diff_w_in.shape[2] // 3
            q = _proj_rope(dm, h, diff_w_in, j, 0, qk, rows_out, tabs, scale=q_scale)
            k = _proj_rope(dm, h, diff_w_in, j, qk, qk, rows_in, tabs)
            v = _proj_plain(dm, h, diff_w_in, j, 2 * qk, qk, rows_in)
            lam_init = 0.8 - 0.6 * math.exp(-0.3 * i)
            y = _attention(dm, q, k, v, kind="diff", ctx_out=not last,
                           params=(diff_lambda[j], diff_subln_g[j].reshape(1, 2 * HEAD_DIM)),
                           lam_init=lam_init)
            w_out = diff_w_out
        elif kind == 1:
            y = _fourier_mix(dm, h, not last)
            w_out = fourier_w_out
        else:
            kvd = (gqa_w_in.shape[2] - d) // 2
            q = _proj_rope(dm, h, gqa_w_in, j, 0, d, rows_out, tabs, scale=q_scale, gain=gqa_qk_g[j, 0])
            k = _proj_rope(dm, h, gqa_w_in, j, d, kvd, rows_in, tabs, gain=gqa_qk_g[j, 1])
            v = _proj_plain(dm, h, gqa_w_in, j, d + kvd, kvd, rows_in)
            y = _attention(dm, q, k, v, kind="gqa", ctx_out=not last)
            w_out = gqa_w_out
        xs = _proj_resid(dm, y, w_out, j, xs, mod, 2, rows_out)

        f = i // 2
        if i % 2 == 0:
            z = _norm_mod(dm, xs, norm_g[i, 1], mod, 3, rows_out)
            hdn = _matmul(dm, z, ffn_w_in, rows=rows_out, mode="swiglu", out_dtype=BF16, e0=f,
                          swiglu_half=ffn_w_in.shape[2] // 2)
            xs = _proj_resid(dm, hdn, ffn_w_out, f, xs, mod, 5, rows_out)
        else:
            z, route, counts = _norm_mod(dm, xs, norm_g[i, 1], mod, 3, rows_out, w_router=moe_router[f])
            slots, src, tile_expert, n_tiles = _route_tables(route, counts, n_e, rows_out)
            zg = _dispatch(z, src, n_tiles)
            hdn = _moe_up(zg, moe_w_in, f * n_e, tile_expert, two_f // 2)
            yg = _moe_down(hdn, moe_w_out, f * n_e, tile_expert)
            xs = _moe_combine(dm, yg, slots, xs, route, mod, 5, rows_out)

    return _final_norm(dm, xs, final_g).reshape(b, dm.seq, d)
```

```python
import functools
import math

import jax
import jax.numpy as jnp
from jax import lax
from jax.experimental import pallas as pl
from jax.experimental.pallas import tpu as pltpu

HEAD_DIM = 128
GRID_W = 64
ROPE_THETA = 10000.0
ROPE_FREQS = HEAD_DIM // 4
NORM_EPS = 1e-6
N_MOD = 6
N_MIXERS = 3
FOURIER_GROUPS = 4
GQA_GROUP = 4

LANES = 128
VMEM_LIMIT_BYTES = 56 * 2**20
ROW_TILE = 512
NORM_ROW_TILE = 256
MOE_TILE = 256
MOE_UP_COLS = 512
COMBINE_ROW_TILE = 128
GATHER_UNROLL = 8
Q_TILE = 256
DIFF_SUBTILES = 2
KEY_CHUNK = 512
LOG2E = math.log2(math.e)
W_TILE_BYTES = 12 * 2**20

F32 = jnp.float32
BF16 = jnp.bfloat16


def _cparams(n_axes):
    return pltpu.CompilerParams(dimension_semantics=("arbitrary",) * n_axes,
                                vmem_limit_bytes=VMEM_LIMIT_BYTES)


def _silu(v):
    return v / (1.0 + jnp.exp(-v))


class _Dims:
    def __init__(self, x, ctx):
        self.b, self.seq, self.d = x.shape
        self.ctx = ctx.shape[1]
        self.m_lat = self.b * self.seq
        self.m_ctx = self.b * self.ctx
        self.m_all = self.m_lat + self.m_ctx
        g = math.gcd(self.seq, self.m_ctx)
        self.tm = min(ROW_TILE, g)
        self.tr = min(NORM_ROW_TILE, g)
        self.tq = min(Q_TILE, self.ctx)
        assert self.seq % self.tq == 0 and self.ctx % self.tq == 0

    def group(self, i, tile):
        r = i * tile
        return jnp.where(r < self.m_lat, r // self.seq, self.b)


def _mod_kernel(c_ref, w_ref, b_ref, o_ref):
    s = _silu(c_ref[...]).astype(BF16)
    w = w_ref[...].astype(BF16)
    o_ref[...] = jnp.dot(s, w, preferred_element_type=F32) + b_ref[...]


def _modulations(cc, w_mod, b_mod):
    depth, d, n = w_mod.shape
    tn = 512 if n % 512 == 0 else n
    rows = cc.shape[0]
    return pl.pallas_call(
        _mod_kernel,
        out_shape=jax.ShapeDtypeStruct((depth, rows, n), F32),
        grid=(depth, n // tn),
        in_specs=[pl.BlockSpec((rows, d), lambda l, j: (0, 0)),
                  pl.BlockSpec((None, d, tn), lambda l, j: (l, 0, j)),
                  pl.BlockSpec((None, 1, tn), lambda l, j: (l, 0, j))],
        out_specs=pl.BlockSpec((None, rows, tn), lambda l, j: (l, 0, j)),
        compiler_params=_cparams(2),
        name="adaln_modulations",
    )(cc, w_mod, b_mod.reshape(depth, 1, n))


def _norm_mod_value(x_ref, g_ref, mod_ref, ci):
    x = x_ref[...]
    y = x * lax.rsqrt(jnp.mean(x * x, axis=-1, keepdims=True) + NORM_EPS) * g_ref[...]
    shift = mod_ref[ci:ci + 1, :]
    scale = mod_ref[ci + 1:ci + 2, :]
    return y * (1.0 + scale) + shift


def _norm_mod_kernel(x_ref, g_ref, mod_ref, o_ref, *, ci):
    o_ref[...] = _norm_mod_value(x_ref, g_ref, mod_ref, ci).astype(BF16)


def _split_bf16(v):
    hi = v.astype(BF16)
    lo = (v - hi.astype(F32)).astype(BF16)
    return hi, lo


ROUTE_COLS = 8


def _norm_mod_router_kernel(x_ref, g_ref, mod_ref, wr_ref, o_ref, route_ref, counts_ref, run_ref, *, ci, n_e):
    z = _norm_mod_value(x_ref, g_ref, mod_ref, ci)
    o_ref[...] = z
    z_hi, z_lo = _split_bf16(z)
    w_hi, w_lo = _split_bf16(wr_ref[...])
    logits = (jnp.dot(z_hi, w_hi, preferred_element_type=F32)
              + jnp.dot(z_lo, w_hi, preferred_element_type=F32)
              + jnp.dot(z_hi, w_lo, preferred_element_type=F32))
    idx = lax.broadcasted_iota(jnp.int32, logits.shape, 1).astype(F32)
    logits = jnp.where(idx < n_e, logits, -jnp.inf)
    m1 = jnp.max(logits, axis=-1, keepdims=True)
    i1 = jnp.min(jnp.where(logits == m1, idx, float(LANES)), axis=-1, keepdims=True)
    rest = jnp.where(idx == i1, -jnp.inf, logits)
    m2 = jnp.max(rest, axis=-1, keepdims=True)
    i2 = jnp.min(jnp.where(rest == m2, idx, float(LANES)), axis=-1, keepdims=True)
    e2 = jnp.exp(m2 - m1)
    w1 = 1.0 / (1.0 + e2)
    w2 = e2 * w1

    @pl.when(pl.program_id(0) == 0)
    def _():
        run_ref[...] = jnp.zeros_like(run_ref)

    sel1, sel2 = idx == i1, idx == i2
    picked = jnp.where(sel1 | sel2, 1.0, 0.0)
    rows = picked.shape[0]
    earlier = (lax.broadcasted_iota(jnp.int32, (rows, rows), 0)
               > lax.broadcasted_iota(jnp.int32, (rows, rows), 1))
    before = jnp.dot(jnp.where(earlier, 1.0, 0.0).astype(BF16), picked.astype(BF16),
                     preferred_element_type=F32) + run_ref[...]
    rank1 = jnp.sum(jnp.where(sel1, before, 0.0), axis=-1, keepdims=True)
    rank2 = jnp.sum(jnp.where(sel2, before, 0.0), axis=-1, keepdims=True)
    run_ref[...] = run_ref[...] + jnp.sum(picked, axis=0, keepdims=True)
    counts_ref[...] = run_ref[...]

    route = jnp.zeros_like(logits)
    for col, val in enumerate((i1, i2, rank1, rank2, w1, w2)):
        route = jnp.where(idx == col, val, route)
    route_ref[...] = route[:, :ROUTE_COLS]


def _norm_mod(dm, x, g, mod, ci, rows, w_router=None):
    d = x.shape[1]
    tr = dm.tr
    in_specs = [pl.BlockSpec((tr, d), lambda i: (i, 0)),
                pl.BlockSpec((1, d), lambda i: (0, 0)),
                pl.BlockSpec((None, N_MOD, d), lambda i: (dm.group(i, tr), 0, 0))]
    h_spec = pl.BlockSpec((tr, d), lambda i: (i, 0))
    h_shape = jax.ShapeDtypeStruct((rows, d), BF16)
    if w_router is None:
        return pl.pallas_call(
            functools.partial(_norm_mod_kernel, ci=ci),
            out_shape=h_shape, grid=(rows // tr,), in_specs=in_specs, out_specs=h_spec,
            compiler_params=_cparams(1), name="norm_modulate",
        )(x, g.reshape(1, d), mod)
    n_e = w_router.shape[1]
    assert n_e <= LANES
    w_router = jnp.pad(w_router, ((0, 0), (0, LANES - n_e)))
    return pl.pallas_call(
        functools.partial(_norm_mod_router_kernel, ci=ci, n_e=n_e),
        out_shape=(jax.ShapeDtypeStruct((rows, d), F32),
                   jax.ShapeDtypeStruct((rows, ROUTE_COLS), F32),
                   jax.ShapeDtypeStruct((1, LANES), F32)),
        grid=(rows // tr,),
        in_specs=in_specs + [pl.BlockSpec((d, LANES), lambda i: (0, 0))],
        out_specs=(h_spec, pl.BlockSpec((tr, ROUTE_COLS), lambda i: (i, 0)),
                   pl.BlockSpec((1, LANES), lambda i: (0, 0))),
        scratch_shapes=[pltpu.VMEM((1, LANES), F32)],
        compiler_params=_cparams(1), name="norm_modulate_router",
    )(x, g.reshape(1, d), mod, w_router)


def _final_norm_kernel(x_ref, g_ref, o_ref):
    x = x_ref[...]
    o_ref[...] = x * lax.rsqrt(jnp.mean(x * x, axis=-1, keepdims=True) + NORM_EPS) * g_ref[...]


def _final_norm(dm, x, g):
    rows, d = x.shape
    tr = dm.tr
    return pl.pallas_call(
        _final_norm_kernel,
        out_shape=jax.ShapeDtypeStruct((rows, d), F32),
        grid=(rows // tr,),
        in_specs=[pl.BlockSpec((tr, d), lambda i: (i, 0)), pl.BlockSpec((1, d), lambda i: (0, 0))],
        out_specs=pl.BlockSpec((tr, d), lambda i: (i, 0)),
        compiler_params=_cparams(1), name="final_norm",
    )(x, g.reshape(1, d))


def _rope_epilogue(acc, cos_ref, sina_ref, sinb_ref, g_ref, o_ref, scale):
    cos, sina, sinb = cos_ref[...], sina_ref[...], sinb_ref[...]
    for c in range(acc.shape[1] // HEAD_DIM):
        xh = acc[:, c * HEAD_DIM:(c + 1) * HEAD_DIM]
        if g_ref is not None:
            xh = xh * lax.rsqrt(jnp.mean(xh * xh, axis=-1, keepdims=True) + NORM_EPS) * g_ref[...]
        fwd = pltpu.roll(xh, HEAD_DIM - ROPE_FREQS, 1)
        bwd = pltpu.roll(xh, ROPE_FREQS, 1)
        r = xh * cos + fwd * sina + bwd * sinb
        if scale != 1.0:
            r = r * scale
        o_ref[:, c * HEAD_DIM:(c + 1) * HEAD_DIM] = r.astype(o_ref.dtype)


def _mm_kernel(*refs, mode, n_a, lat_tiles, n_w, cast, n_extra, gate_idx, scale):
    a_refs = refs[:n_a]
    w_refs = refs[n_a:n_a + n_w]
    extra = refs[n_a + n_w:n_a + n_w + n_extra]
    o_ref = refs[n_a + n_w + n_extra]
    wb_refs = refs[n_a + n_w + n_extra + 1:]

    if cast:
        @pl.when(pl.program_id(1) == 0)
        def _():
            for w_ref, wb_ref in zip(w_refs, wb_refs):
                wb_ref[...] = w_ref[...].astype(BF16)
        w_srcs = wb_refs
    else:
        w_srcs = w_refs

    if n_a == 1:
        _mm_tile(a_refs[0], w_srcs, extra, o_ref, mode, gate_idx, scale)
        return

    i = pl.program_id(1)

    @pl.when(i < lat_tiles)
    def _():
        _mm_tile(a_refs[0], w_srcs, extra, o_ref, mode, gate_idx, scale)

    @pl.when(i >= lat_tiles)
    def _():
        _mm_tile(a_refs[1], w_srcs, extra, o_ref, mode, gate_idx, scale)


def _mm_tile(a_ref, w_srcs, extra, o_ref, mode, gate_idx, scale):
    a = a_ref[...]
    accs = [jnp.dot(a, w[...], preferred_element_type=F32) for w in w_srcs]
    n_extra = len(extra)

    if mode == "plain":
        o_ref[...] = accs[0].astype(o_ref.dtype)
    elif mode == "rope":
        g_ref = extra[3] if n_extra == 4 else None
        _rope_epilogue(accs[0], extra[0], extra[1], extra[2], g_ref, o_ref, scale)
    elif mode == "resid":
        x_ref, mod_ref = extra
        gate = mod_ref[gate_idx:gate_idx + 1, :]
        o_ref[...] = x_ref[...] + gate * accs[0]
    elif mode == "swiglu":
        o_ref[...] = (_silu(accs[0]) * accs[1]).astype(o_ref.dtype)
    else:
        raise ValueError(mode)


def _pick_tn(k, n, itemsize, n_w):
    best = None
    for tn in range(LANES, n + 1, LANES):
        if n % tn == 0 and k * tn * itemsize * n_w <= W_TILE_BYTES:
            best = tn
    return best if best is not None else (LANES if n % LANES == 0 else n)


def _matmul(dm, a, w, *, rows, mode, out_dtype, e0=0, n_e=1, col0=0, n_cols=None, tn=None,
            extra=(), extra_specs=(), gate_idx=0, scale=1.0, swiglu_half=None, a_groups=False):
    a_parts = list(a) if isinstance(a, (list, tuple)) else [a]
    _, k, n_w_cols = w.shape
    assert all(p.shape[1] == (n_e * k if a_groups else k) for p in a_parts)
    cast = w.dtype != BF16
    n_w = 2 if mode == "swiglu" else 1
    if n_cols is None:
        n_cols = swiglu_half if mode == "swiglu" else n_w_cols - col0
    if tn is None:
        tn = _pick_tn(k, n_cols, w.dtype.itemsize, n_w)
    tm = dm.tm
    assert n_cols % tn == 0 and col0 % tn == 0 and rows % tm == 0
    bpe = n_cols // tn
    blk0 = col0 // tn

    def w_map(off):
        if a_groups:
            return lambda j, i: (e0, 0, blk0 + off + j % bpe)
        return lambda j, i: (e0 + j // bpe, 0, blk0 + off + j % bpe)

    lat_tiles = a_parts[0].shape[0] // tm
    if len(a_parts) == 1:
        a_maps = [(lambda j, i: (i, j // bpe)) if a_groups else (lambda j, i: (i, 0))]
    else:
        assert not a_groups and a_parts[0].shape[0] % tm == 0
        assert rows == a_parts[0].shape[0] + a_parts[1].shape[0]
        a_maps = [lambda j, i: (jnp.minimum(i, lat_tiles - 1), 0),
                  lambda j, i: (jnp.maximum(i - lat_tiles, 0), 0)]
    in_specs = [pl.BlockSpec((tm, k), m) for m in a_maps] + [pl.BlockSpec((None, k, tn), w_map(0))]
    args = a_parts + [w]
    if n_w == 2:
        assert swiglu_half % tn == 0
        in_specs.append(pl.BlockSpec((None, k, tn), w_map(swiglu_half // tn)))
        args.append(w)
    in_specs += list(extra_specs)
    args += list(extra)
    scratch = [pltpu.VMEM((k, tn), BF16) for _ in range(n_w)] if cast else []
    kern = functools.partial(_mm_kernel, mode=mode, n_a=len(a_parts), lat_tiles=lat_tiles, n_w=n_w,
                             cast=cast, n_extra=len(extra), gate_idx=gate_idx, scale=scale)
    return pl.pallas_call(
        kern,
        out_shape=jax.ShapeDtypeStruct((rows, n_e * n_cols), out_dtype),
        grid=(n_e * bpe, rows // tm),
        in_specs=in_specs,
        out_specs=pl.BlockSpec((tm, tn), lambda j, i: (i, j)),
        scratch_shapes=scratch,
        compiler_params=_cparams(2),
        name="matmul_" + mode,
    )(*args)


def _proj_rope(dm, h, w, layer, col0, n_cols, rows, tabs, scale=1.0, gain=None):
    tm = dm.tm
    tab_spec = pl.BlockSpec((tm, HEAD_DIM), lambda j, i: (i, 0))
    extra, specs = list(tabs), [tab_spec] * 3
    if gain is not None:
        extra.append(gain.reshape(1, HEAD_DIM))
        specs.append(pl.BlockSpec((1, HEAD_DIM), lambda j, i: (0, 0)))
    return _matmul(dm, h, w, rows=rows, mode="rope", out_dtype=BF16, e0=layer, col0=col0,
                   n_cols=n_cols, extra=extra, extra_specs=specs, scale=scale)


def _proj_plain(dm, h, w, layer, col0, n_cols, rows):
    return _matmul(dm, h, w, rows=rows, mode="plain", out_dtype=BF16, e0=layer, col0=col0,
                   n_cols=n_cols)


def _proj_resid(dm, a, w, layer, x, mod, gate_idx, rows):
    tm = dm.tm
    _, k, n = w.shape
    tn = _pick_tn(k, n, w.dtype.itemsize, 1)
    specs = [pl.BlockSpec((tm, tn), lambda j, i: (i, j)),
             pl.BlockSpec((None, N_MOD, tn), lambda j, i: (dm.group(i, tm), 0, j))]
    return _matmul(dm, a, w, rows=rows, mode="resid", out_dtype=F32, e0=layer, tn=tn,
                   extra=[x, mod], extra_specs=specs, gate_idx=gate_idx)


def _route_tables(route, counts, n_e, rows):
    tg = MOE_TILE
    expert = route[:, 0:2].astype(jnp.int32)
    rank = route[:, 2:4].astype(jnp.int32)
    cnt = counts[0, :n_e].astype(jnp.int32)
    padded = (cnt + tg - 1) // tg * tg
    ends = jnp.cumsum(padded)
    slots = (ends - padded)[expert] + rank
    assert (2 * rows) % tg == 0
    n_tiles = 2 * rows // tg + n_e
    tok = jnp.broadcast_to(jnp.arange(rows, dtype=jnp.int32)[:, None], (rows, 2))
    src = jnp.zeros((n_tiles * tg,), jnp.int32).at[slots.reshape(-1)].set(tok.reshape(-1))
    tile_row0 = jnp.arange(n_tiles, dtype=jnp.int32)[:, None] * tg
    tile_expert = jnp.minimum(jnp.sum(tile_row0 >= ends[None, :], axis=1), n_e - 1).astype(jnp.int32)
    return slots, src, tile_expert, n_tiles


def _rows_copy(src_hbm, dst, sem, n):
    return pltpu.make_async_copy(src_hbm.at[pl.ds(0, n), :], dst, sem)


def _gather_rows(idx_ref, stride, offset, src_hbm, dst_ref, sem):
    def issue(r, carry):
        row = idx_ref[0, stride * r + offset]
        pltpu.make_async_copy(src_hbm.at[pl.ds(row, 1), :], dst_ref.at[pl.ds(r, 1), :], sem).start()
        return carry

    lax.fori_loop(0, dst_ref.shape[0], issue, 0, unroll=GATHER_UNROLL)


def _dispatch_kernel(cur_ref, nxt_ref, z_hbm, o_ref, buf_ref, sem):
    g = pl.program_id(0)
    slot = g % 2

    @pl.when(g == 0)
    def _():
        _gather_rows(cur_ref, 1, 0, z_hbm, buf_ref.at[0], sem.at[0])

    @pl.when(g + 1 < pl.num_programs(0))
    def _():
        _gather_rows(nxt_ref, 1, 0, z_hbm, buf_ref.at[1 - slot], sem.at[1 - slot])

    _rows_copy(z_hbm, buf_ref.at[slot], sem.at[slot], buf_ref.shape[1]).wait()
    o_ref[...] = buf_ref[slot].astype(o_ref.dtype)


def _dispatch(z, src, n_tiles):
    tg = MOE_TILE
    d = z.shape[1]
    src = src.reshape(n_tiles, 1, tg)
    return pl.pallas_call(
        _dispatch_kernel,
        out_shape=jax.ShapeDtypeStruct((n_tiles * tg, d), BF16),
        grid=(n_tiles,),
        in_specs=[pl.BlockSpec((None, 1, tg), lambda g: (g, 0, 0), memory_space=pltpu.SMEM),
                  pl.BlockSpec((None, 1, tg), lambda g: (jnp.minimum(g + 1, n_tiles - 1), 0, 0),
                               memory_space=pltpu.SMEM),
                  pl.BlockSpec(memory_space=pl.ANY)],
        out_specs=pl.BlockSpec((tg, d), lambda g: (g, 0)),
        scratch_shapes=[pltpu.VMEM((2, tg, d), z.dtype), pltpu.SemaphoreType.DMA((2,))],
        compiler_params=_cparams(1), name="moe_dispatch",
    )(src, src, z)


def _expert_changed(te_ref):
    g = pl.program_id(1)
    return (g == 0) | (te_ref[g] != te_ref[jnp.maximum(g - 1, 0)])


def _moe_up_kernel(te_ref, a_ref, wg_ref, wu_ref, o_ref, wgb_ref, wub_ref):
    @pl.when(_expert_changed(te_ref))
    def _():
        wgb_ref[...] = wg_ref[...].astype(BF16)
        wub_ref[...] = wu_ref[...].astype(BF16)

    a = a_ref[...]
    hg = jnp.dot(a, wgb_ref[...], preferred_element_type=F32)
    hu = jnp.dot(a, wub_ref[...], preferred_element_type=F32)
    o_ref[...] = (_silu(hg) * hu).astype(o_ref.dtype)


def _moe_down_kernel(te_ref, a_ref, w_ref, o_ref, wb_ref):
    @pl.when(_expert_changed(te_ref))
    def _():
        wb_ref[...] = w_ref[...].astype(BF16)

    o_ref[...] = jnp.dot(a_ref[...], wb_ref[...], preferred_element_type=F32)


def _moe_up(zg, w_in, e0, tile_expert, f):
    tg = MOE_TILE
    s, d = zg.shape
    tn = min(f, MOE_UP_COLS)
    assert f % tn == 0
    nb = f // tn
    grid_spec = pltpu.PrefetchScalarGridSpec(
        num_scalar_prefetch=1, grid=(nb, s // tg),
        in_specs=[pl.BlockSpec((tg, d), lambda j, g, te: (g, 0)),
                  pl.BlockSpec((None, d, tn), lambda j, g, te: (e0 + te[g], 0, j)),
                  pl.BlockSpec((None, d, tn), lambda j, g, te: (e0 + te[g], 0, nb + j))],
        out_specs=pl.BlockSpec((tg, tn), lambda j, g, te: (g, j)),
        scratch_shapes=[pltpu.VMEM((d, tn), BF16), pltpu.VMEM((d, tn), BF16)])
    return pl.pallas_call(
        _moe_up_kernel, out_shape=jax.ShapeDtypeStruct((s, f), BF16), grid_spec=grid_spec,
        compiler_params=_cparams(2), name="moe_up",
    )(tile_expert, zg, w_in, w_in)


def _moe_down(hdn, w_out, e0, tile_expert):
    tg = MOE_TILE
    s, f = hdn.shape
    d = w_out.shape[2]
    tn = _pick_tn(f, d, 4, 1)
    grid_spec = pltpu.PrefetchScalarGridSpec(
        num_scalar_prefetch=1, grid=(d // tn, s // tg),
        in_specs=[pl.BlockSpec((tg, f), lambda j, g, te: (g, 0)),
                  pl.BlockSpec((None, f, tn), lambda j, g, te: (e0 + te[g], 0, j))],
        out_specs=pl.BlockSpec((tg, tn), lambda j, g, te: (g, j)),
        scratch_shapes=[pltpu.VMEM((f, tn), BF16)])
    return pl.pallas_call(
        _moe_down_kernel, out_shape=jax.ShapeDtypeStruct((s, d), F32), grid_spec=grid_spec,
        compiler_params=_cparams(2), name="moe_down",
    )(tile_expert, hdn, w_out)


def _combine_kernel(cur_ref, nxt_ref, y_hbm, x_ref, route_ref, mod_ref, o_ref, y_ref, sem, *, gate_idx):
    i = pl.program_id(0)
    slot = i % 2
    tr = x_ref.shape[0]

    def start(idx_ref, s):
        for c in range(2):
            _gather_rows(idx_ref, 2, c, y_hbm, y_ref.at[s, c], sem.at[s, c])

    @pl.when(i == 0)
    def _():
        start(cur_ref, 0)

    @pl.when(i + 1 < pl.num_programs(0))
    def _():
        start(nxt_ref, 1 - slot)

    for c in range(2):
        _rows_copy(y_hbm, y_ref.at[slot, c], sem.at[slot, c], tr).wait()
    route = route_ref[...]
    gate = mod_ref[gate_idx:gate_idx + 1, :]
    o_ref[...] = x_ref[...] + gate * (route[:, 4:5] * y_ref[slot, 0] + route[:, 5:6] * y_ref[slot, 1])


def _moe_combine(dm, yg, slots, x, route, mod, gate_idx, rows):
    tr = COMBINE_ROW_TILE
    d = x.shape[1]
    n = rows // tr
    slots = slots.reshape(n, 1, 2 * tr)
    return pl.pallas_call(
        functools.partial(_combine_kernel, gate_idx=gate_idx),
        out_shape=jax.ShapeDtypeStruct((rows, d), F32),
        grid=(n,),
        in_specs=[pl.BlockSpec((None, 1, 2 * tr), lambda i: (i, 0, 0), memory_space=pltpu.SMEM),
                  pl.BlockSpec((None, 1, 2 * tr), lambda i: (jnp.minimum(i + 1, n - 1), 0, 0),
                               memory_space=pltpu.SMEM),
                  pl.BlockSpec(memory_space=pl.ANY),
                  pl.BlockSpec((tr, d), lambda i: (i, 0)),
                  pl.BlockSpec((tr, ROUTE_COLS), lambda i: (i, 0)),
                  pl.BlockSpec((None, N_MOD, d), lambda i: (dm.group(i, tr), 0, 0))],
        out_specs=pl.BlockSpec((tr, d), lambda i: (i, 0)),
        scratch_shapes=[pltpu.VMEM((2, 2, tr, d), F32), pltpu.SemaphoreType.DMA((2, 2))],
        compiler_params=_cparams(1), name="moe_combine",
    )(slots, slots, yg, x, route, mod)


_NT = (((1,), (1,)), ((), ()))


def _key_chunks(refs):
    out, off = [], 0
    for r in refs:
        n = r.shape[0]
        ck = min(KEY_CHUNK, n)
        for r0 in range(0, n, ck):
            out.append((r, r0, ck, off))
            off += ck
    return out


def _lane_fold(acc, v, op):
    for t in range(v.shape[1] // LANES):
        piece = v[:, t * LANES:(t + 1) * LANES]
        acc = piece if acc is None else op(acc, piece)
    return acc


def _scores_pass(q, k_refs, lo, hi, s_ref):
    mx = None
    for r, r0, ck, off in _key_chunks(k_refs):
        s = lax.dot_general(q, r[r0:r0 + ck, lo:hi], _NT, preferred_element_type=F32)
        s_ref[:, off:off + ck] = s
        mx = _lane_fold(mx, s, jnp.maximum)
    return jnp.max(mx, axis=-1, keepdims=True)


def _values_pass(s_ref, m, v_refs, lo, hi):
    acc, ls = None, None
    for r, r0, ck, off in _key_chunks(v_refs):
        e = jnp.exp2(s_ref[:, off:off + ck] - m)
        ls = _lane_fold(ls, e, jnp.add)
        pv = jnp.dot(e.astype(BF16), r[r0:r0 + ck, lo:hi], preferred_element_type=F32)
        acc = pv if acc is None else acc + pv
    return acc * (1.0 / jnp.sum(ls, axis=-1, keepdims=True))


def _diff_attn_body(q_ref, k_refs, v_refs, lam_ref, g_ref, o_ref, s_refs, sub, lam_init):
    lv = lam_ref[...]
    lam = (jnp.exp(jnp.sum(lv[0:1] * lv[1:2], axis=-1, keepdims=True))
           - jnp.exp(jnp.sum(lv[2:3] * lv[3:4], axis=-1, keepdims=True)) + lam_init)
    n_sub = q_ref.shape[0] // sub
    chains = [(t, c) for t in range(n_sub) for c in range(2)]
    m = {}
    for t, c in chains:
        lo, hi = c * HEAD_DIM, (c + 1) * HEAD_DIM
        m[t, c] = _scores_pass(q_ref[t * sub:(t + 1) * sub, lo:hi], k_refs, lo, hi, s_refs[2 * t + c])
    outs = {ch: _values_pass(s_refs[2 * ch[0] + ch[1]], m[ch], v_refs, 0, 2 * HEAD_DIM) for ch in chains}
    for t in range(n_sub):
        o = outs[t, 0] - lam * outs[t, 1]
        o = o * lax.rsqrt(jnp.mean(o * o, axis=-1, keepdims=True) + NORM_EPS) * g_ref[...]
        o_ref[t * sub:(t + 1) * sub, :] = (o * (1.0 - lam_init)).astype(o_ref.dtype)


def _gqa_attn_body(q_ref, k_refs, v_refs, o_ref, s_refs):
    m = [_scores_pass(q_ref[:, g * HEAD_DIM:(g + 1) * HEAD_DIM], k_refs, 0, HEAD_DIM, s_refs[g])
         for g in range(GQA_GROUP)]
    for g in range(GQA_GROUP):
        o = _values_pass(s_refs[g], m[g], v_refs, 0, HEAD_DIM)
        o_ref[:, g * HEAD_DIM:(g + 1) * HEAD_DIM] = o.astype(o_ref.dtype)


def _attn_kernel(q_ref, *rest, kind, n_src, n_chains, sub, lam_init):
    k_refs = rest[:n_src]
    v_refs = rest[n_src:2 * n_src]
    params = rest[2 * n_src:-n_chains - 1]
    o_ref = rest[-n_chains - 1]
    s_refs = rest[-n_chains:]
    if kind == "diff":
        _diff_attn_body(q_ref, k_refs, v_refs, params[0], params[1], o_ref, s_refs, sub, lam_init)
    else:
        _gqa_attn_body(q_ref, k_refs, v_refs, o_ref, s_refs)


def _attention_call(dm, q, k, v, *, kind, latent, params, lam_init):
    sub = dm.tq
    n_sub = DIFF_SUBTILES if (kind == "diff" and latent and dm.seq % (DIFF_SUBTILES * sub) == 0) else 1
    tq = sub * n_sub
    wq = 2 * HEAD_DIM if kind == "diff" else GQA_GROUP * HEAD_DIM
    wk = 2 * HEAD_DIM if kind == "diff" else HEAD_DIM
    n_heads = k.shape[1] // wk
    lat_per_ctx = dm.m_lat // dm.ctx
    lat_spec = pl.BlockSpec((dm.seq, wk), lambda b, h, qi: (b, h))
    ctx_spec = pl.BlockSpec((dm.ctx, wk), lambda b, h, qi: (lat_per_ctx + b, h))
    if latent:
        n_q, row0, srcs, n_keys = dm.seq // tq, 0, [lat_spec, ctx_spec], dm.seq + dm.ctx
    else:
        n_q, row0, srcs, n_keys = dm.ctx // tq, dm.m_lat // tq, [ctx_spec], dm.ctx
    q_spec = pl.BlockSpec((tq, wq), lambda b, h, qi: (row0 + b * n_q + qi, h))
    in_specs = [q_spec] + srcs + srcs + [pl.BlockSpec(p.shape, lambda b, h, qi: (0, 0)) for p in params]
    args = [q] + [k] * len(srcs) + [v] * len(srcs) + list(params)
    n_chains = (2 if kind == "diff" else GQA_GROUP) * n_sub
    return pl.pallas_call(
        functools.partial(_attn_kernel, kind=kind, n_src=len(srcs), n_chains=n_chains, sub=sub,
                          lam_init=lam_init),
        out_shape=jax.ShapeDtypeStruct((dm.b * n_q * tq, q.shape[1]), BF16),
        grid=(dm.b, n_heads, n_q),
        in_specs=in_specs,
        out_specs=pl.BlockSpec((tq, wq), lambda b, h, qi: (b * n_q + qi, h)),
        scratch_shapes=[pltpu.VMEM((sub, n_keys), F32) for _ in range(n_chains)],
        compiler_params=_cparams(3),
        name=kind + ("_attention" if latent else "_attention_ctx"),
    )(*args)


def _attention(dm, q, k, v, *, kind, ctx_out, params=(), lam_init=0.0):
    common = dict(kind=kind, params=params, lam_init=lam_init)
    y = [_attention_call(dm, q, k, v, latent=True, **common)]
    if ctx_out:
        y.append(_attention_call(dm, q, k, v, latent=False, **common))
    return y


def _dft_tables(n, norm):
    n0 = 1
    while n0 * n0 * 4 <= n and n % (n0 * 2) == 0:
        n0 *= 2
    n1 = n // n0
    k = jnp.arange(n, dtype=jnp.int32)[:, None]

    def cs(m):
        ang = ((k * m) % n).astype(F32) * (2.0 * math.pi / n)
        return jnp.cos(ang), jnp.sin(ang)

    c1, s1 = cs(jnp.arange(n1, dtype=jnp.int32)[None, :] * n0)
    c0, s0 = cs(jnp.arange(n0, dtype=jnp.int32)[None, :])
    c1, s1, c0, s0 = c1[:, :, None], s1[:, :, None], c0[:, None, :], s0[:, None, :]
    cos = (c1 * c0 - s1 * s0).reshape(n, n) * norm
    sin = (s1 * c0 + c1 * s0).reshape(n, n) * norm
    return cos, sin


def _dft_rows_kernel(c_ref, s_ref, yc_ref, ys_ref, o_ref):
    o_ref[...] = (jnp.dot(c_ref[...], yc_ref[...], preferred_element_type=F32)
                  + jnp.dot(s_ref[...], ys_ref[...], preferred_element_type=F32)).astype(o_ref.dtype)


def _dft_rows(dm, y, n, row0):
    d = dm.d
    dg = d // FOURIER_GROUPS
    cos, sin = _dft_tables(n, n ** -0.5)
    cos, nsin = cos.astype(BF16), (-sin).astype(BF16)
    tmf = min(ROW_TILE, n)
    tn = min(512, dg)
    lb = dg // tn
    rb0 = row0 // n

    def y_map(off):
        return lambda b, j, i: (rb0 + b, (j // lb) * 2 * lb + off + j % lb)

    in_specs = [pl.BlockSpec((tmf, n), lambda b, j, i: (i, 0)),
                pl.BlockSpec((tmf, n), lambda b, j, i: (i, 0)),
                pl.BlockSpec((n, tn), y_map(0)),
                pl.BlockSpec((n, tn), y_map(lb))]
    tiles = n // tmf
    return pl.pallas_call(
        _dft_rows_kernel,
        out_shape=jax.ShapeDtypeStruct((dm.b * n, d), BF16),
        grid=(dm.b, d // tn, tiles),
        in_specs=in_specs,
        out_specs=pl.BlockSpec((tmf, tn), lambda b, j, i: (b * tiles + i, j)),
        compiler_params=_cparams(3),
        name="dft_positions",
    )(cos, nsin, y, y)


def _fourier_mix(dm, h, ctx_out):
    d = dm.d
    dg = d // FOURIER_GROUPS
    rows = h.shape[0]
    cos_c, sin_c = _dft_tables(dg, dg ** -0.5)
    cs = jnp.concatenate([cos_c, sin_c], axis=1).astype(BF16)[None]
    y = _matmul(dm, h, cs, rows=rows, mode="plain", out_dtype=BF16, n_e=FOURIER_GROUPS,
                a_groups=True, tn=min(2 * dg, 1024))
    f = [_dft_rows(dm, y, dm.seq, 0)]
    if ctx_out:
        f.append(_dft_rows(dm, y, dm.ctx, dm.m_lat))
    return f


def _rope_tables(dm):
    rows = dm.seq // GRID_W
    r, col = jnp.meshgrid(jnp.arange(rows), jnp.arange(GRID_W), indexing="ij")
    pos = jnp.stack([r.reshape(-1), col.reshape(-1)], axis=-1).astype(F32)
    inv_freq = 1.0 / (ROPE_THETA ** (jnp.arange(ROPE_FREQS, dtype=F32) / ROPE_FREQS))
    ang = pos[:, :, None] * inv_freq
    cos, sin = jnp.cos(ang), jnp.sin(ang)
    zero = jnp.zeros_like(sin)
    cos_t = jnp.stack([cos, cos], axis=2).reshape(dm.seq, HEAD_DIM)
    sina_t = jnp.stack([-sin, zero], axis=2).reshape(dm.seq, HEAD_DIM)
    sinb_t = jnp.stack([zero, sin], axis=2).reshape(dm.seq, HEAD_DIM)

    def full(t, fill):
        return jnp.concatenate([jnp.tile(t, (dm.b, 1)), jnp.full((dm.m_ctx, HEAD_DIM), fill, F32)], axis=0)

    return full(cos_t, 1.0), full(sina_t, 0.0), full(sinb_t, 0.0)


def kernel(x, c, ctx, c_ctx, w_mod, b_mod, norm_g, diff_w_in, diff_w_out, diff_lambda, diff_subln_g,
           fourier_w_out, gqa_w_in, gqa_w_out, gqa_qk_g, ffn_w_in, ffn_w_out, moe_router, moe_w_in,
           moe_w_out, final_g):
    dm = _Dims(x, ctx)
    b, d = dm.b, dm.d
    depth = w_mod.shape[0]
    q_scale = HEAD_DIM ** -0.5 * LOG2E

    cc = jnp.concatenate([c, c_ctx[None, :], jnp.zeros((8 - b - 1, d), F32)], axis=0)
    mods = _modulations(cc, w_mod, b_mod)[:, :b + 1].reshape(depth, b + 1, N_MOD, d)
    tabs = _rope_tables(dm)
    xs = jnp.concatenate([x.reshape(dm.m_lat, d), ctx.reshape(dm.m_ctx, d)], axis=0)

    n_moe, n_e, _, two_f = moe_w_in.shape
    moe_w_in = moe_w_in.reshape(n_moe * n_e, d, two_f)
    moe_w_out = moe_w_out.reshape(n_moe * n_e, two_f // 2, d)

    for i in range(depth):
        last = i == depth - 1
        mod = mods[i]
        rows_in = xs.shape[0]
        rows_out = dm.m_lat if last else dm.m_all
        h = _norm_mod(dm, xs, norm_g[i, 0], mod, 0, rows_in)
        kind, j = i % N_MIXERS, i // N_MIXERS
        if kind == 0:
            qk = diff_w_in.shape[2] // 3
            q = _proj_rope(dm, h, diff_w_in, j, 0, qk, rows_out, tabs, scale=q_scale)
            k = _proj_rope(dm, h, diff_w_in, j, qk, qk, rows_in, tabs)
            v = _proj_plain(dm, h, diff_w_in, j, 2 * qk, qk, rows_in)
            lam_init = 0.8 - 0.6 * math.exp(-0.3 * i)
            y = _attention(dm, q, k, v, kind="diff", ctx_out=not last,
                           params=(diff_lambda[j], diff_subln_g[j].reshape(1, 2 * HEAD_DIM)),
                           lam_init=lam_init)
            w_out = diff_w_out
        elif kind == 1:
            y = _fourier_mix(dm, h, not last)
            w_out = fourier_w_out
        else:
            kvd = (gqa_w_in.shape[2] - d) // 2
            q = _proj_rope(dm, h, gqa_w_in, j, 0, d, rows_out, tabs, scale=q_scale, gain=gqa_qk_g[j, 0])
            k = _proj_rope(dm, h, gqa_w_in, j, d, kvd, rows_in, tabs, gain=gqa_qk_g[j, 1])
            v = _proj_plain(dm, h, gqa_w_in, j, d + kvd, kvd, rows_in)
            y = _attention(dm, q, k, v, kind="gqa", ctx_out=not last)
            w_out = gqa_w_out
        xs = _proj_resid(dm, y, w_out, j, xs, mod, 2, rows_out)

        f = i // 2
        if i % 2 == 0:
            z = _norm_mod(dm, xs, norm_g[i, 1], mod, 3, rows_out)
            hdn = _matmul(dm, z, ffn_w_in, rows=rows_out, mode="swiglu", out_dtype=BF16, e0=f,
                          swiglu_half=ffn_w_in.shape[2] // 2)
            xs = _proj_resid(dm, hdn, ffn_w_out, f, xs, mod, 5, rows_out)
        else:
            z, route, counts = _norm_mod(dm, xs, norm_g[i, 1], mod, 3, rows_out, w_router=moe_router[f])
            slots, src, tile_expert, n_tiles = _route_tables(route, counts, n_e, rows_out)
            zg = _dispatch(z, src, n_tiles)
            hdn = _moe_up(zg, moe_w_in, f * n_e, tile_expert, two_f // 2)
            yg = _moe_down(hdn, moe_w_out, f * n_e, tile_expert)
            xs = _moe_combine(dm, yg, slots, xs, route, mod, 5, rows_out)

    return _final_norm(dm, xs, final_g).reshape(b, dm.seq, d)
```

```python
import functools
import math

import jax
import jax.numpy as jnp
from jax import lax
from jax.experimental import pallas as pl
from jax.experimental.pallas import tpu as pltpu

HEAD_DIM = 128
GRID_W = 64
ROPE_THETA = 10000.0
ROPE_FREQS = HEAD_DIM // 4
NORM_EPS = 1e-6
N_MOD = 6
N_MIXERS = 3
FOURIER_GROUPS = 4
GQA_GROUP = 4

LANES = 128
VMEM_LIMIT_BYTES = 56 * 2**20
ROW_TILE = 1024
DFT_ROW_TILE = 512
VMEM_TILE_BUDGET = 46 * 2**20
NORM_ROW_TILE = 256
MOE_TILE = 512
MOE_UP_COLS = 512
COMBINE_ROW_TILE = 128
GATHER_UNROLL = 8
Q_TILE = 256
DIFF_SUBTILES = 2
KEY_CHUNK = 512
LOG2E = math.log2(math.e)
W_TILE_BYTES = 12 * 2**20

F32 = jnp.float32
BF16 = jnp.bfloat16


def _cparams(n_axes):
    return pltpu.CompilerParams(dimension_semantics=("arbitrary",) * n_axes,
                                vmem_limit_bytes=VMEM_LIMIT_BYTES)


def _silu(v):
    return v / (1.0 + jnp.exp(-v))


class _Dims:
    def __init__(self, x, ctx):
        self.b, self.seq, self.d = x.shape
        self.ctx = ctx.shape[1]
        self.m_lat = self.b * self.seq
        self.m_ctx = self.b * self.ctx
        self.m_all = self.m_lat + self.m_ctx
        g = math.gcd(self.seq, self.m_ctx)
        self.tm = math.gcd(ROW_TILE, self.seq)
        self.tr = min(NORM_ROW_TILE, g)
        self.tq = min(Q_TILE, self.ctx)
        assert self.seq % self.tq == 0 and self.ctx % self.tq == 0

    def group(self, i, tile):
        r = i * tile
        return jnp.where(r < self.m_lat, r // self.seq, self.b)


def _mod_kernel(c_ref, w_ref, b_ref, o_ref):
    s = _silu(c_ref[...]).astype(BF16)
    w = w_ref[...].astype(BF16)
    o_ref[...] = jnp.dot(s, w, preferred_element_type=F32) + b_ref[...]


def _modulations(cc, w_mod, b_mod):
    depth, d, n = w_mod.shape
    tn = 512 if n % 512 == 0 else n
    rows = cc.shape[0]
    return pl.pallas_call(
        _mod_kernel,
        out_shape=jax.ShapeDtypeStruct((depth, rows, n), F32),
        grid=(depth, n // tn),
        in_specs=[pl.BlockSpec((rows, d), lambda l, j: (0, 0)),
                  pl.BlockSpec((None, d, tn), lambda l, j: (l, 0, j)),
                  pl.BlockSpec((None, 1, tn), lambda l, j: (l, 0, j))],
        out_specs=pl.BlockSpec((None, rows, tn), lambda l, j: (l, 0, j)),
        compiler_params=_cparams(2),
        name="adaln_modulations",
    )(cc, w_mod, b_mod.reshape(depth, 1, n))


def _norm_mod_value(x_ref, g_ref, mod_ref, ci):
    x = x_ref[...]
    y = x * lax.rsqrt(jnp.mean(x * x, axis=-1, keepdims=True) + NORM_EPS) * g_ref[...]
    shift = mod_ref[ci:ci + 1, :]
    scale = mod_ref[ci + 1:ci + 2, :]
    return y * (1.0 + scale) + shift


def _norm_mod_kernel(x_ref, g_ref, mod_ref, o_ref, *, ci):
    o_ref[...] = _norm_mod_value(x_ref, g_ref, mod_ref, ci).astype(BF16)


def _split_bf16(v):
    hi = v.astype(BF16)
    lo = (v - hi.astype(F32)).astype(BF16)
    return hi, lo


ROUTE_COLS = 8


def _norm_mod_router_kernel(x_ref, g_ref, mod_ref, wr_ref, o_ref, route_ref, counts_ref, run_ref, *, ci, n_e):
    z = _norm_mod_value(x_ref, g_ref, mod_ref, ci)
    o_ref[...] = z
    z_hi, z_lo = _split_bf16(z)
    w_hi, w_lo = _split_bf16(wr_ref[...])
    logits = (jnp.dot(z_hi, w_hi, preferred_element_type=F32)
              + jnp.dot(z_lo, w_hi, preferred_element_type=F32)
              + jnp.dot(z_hi, w_lo, preferred_element_type=F32))
    idx = lax.broadcasted_iota(jnp.int32, logits.shape, 1).astype(F32)
    logits = jnp.where(idx < n_e, logits, -jnp.inf)
    m1 = jnp.max(logits, axis=-1, keepdims=True)
    i1 = jnp.min(jnp.where(logits == m1, idx, float(LANES)), axis=-1, keepdims=True)
    rest = jnp.where(idx == i1, -jnp.inf, logits)
    m2 = jnp.max(rest, axis=-1, keepdims=True)
    i2 = jnp.min(jnp.where(rest == m2, idx, float(LANES)), axis=-1, keepdims=True)
    e2 = jnp.exp(m2 - m1)
    w1 = 1.0 / (1.0 + e2)
    w2 = e2 * w1

    @pl.when(pl.program_id(0) == 0)
    def _():
        run_ref[...] = jnp.zeros_like(run_ref)

    sel1, sel2 = idx == i1, idx == i2
    picked = jnp.where(sel1 | sel2, 1.0, 0.0)
    rows = picked.shape[0]
    earlier = (lax.broadcasted_iota(jnp.int32, (rows, rows), 0)
               > lax.broadcasted_iota(jnp.int32, (rows, rows), 1))
    before = jnp.dot(jnp.where(earlier, 1.0, 0.0).astype(BF16), picked.astype(BF16),
                     preferred_element_type=F32) + run_ref[...]
    rank1 = jnp.sum(jnp.where(sel1, before, 0.0), axis=-1, keepdims=True)
    rank2 = jnp.sum(jnp.where(sel2, before, 0.0), axis=-1, keepdims=True)
    run_ref[...] = run_ref[...] + jnp.sum(picked, axis=0, keepdims=True)
    counts_ref[...] = run_ref[...]

    route = jnp.zeros_like(logits)
    for col, val in enumerate((i1, i2, rank1, rank2, w1, w2)):
        route = jnp.where(idx == col, val, route)
    route_ref[...] = route[:, :ROUTE_COLS]


def _norm_mod(dm, x, g, mod, ci, rows, w_router=None):
    d = x.shape[1]
    tr = dm.tr
    in_specs = [pl.BlockSpec((tr, d), lambda i: (i, 0)),
                pl.BlockSpec((1, d), lambda i: (0, 0)),
                pl.BlockSpec((None, N_MOD, d), lambda i: (dm.group(i, tr), 0, 0))]
    h_spec = pl.BlockSpec((tr, d), lambda i: (i, 0))
    h_shape = jax.ShapeDtypeStruct((rows, d), BF16)
    if w_router is None:
        return pl.pallas_call(
            functools.partial(_norm_mod_kernel, ci=ci),
            out_shape=h_shape, grid=(rows // tr,), in_specs=in_specs, out_specs=h_spec,
            compiler_params=_cparams(1), name="norm_modulate",
        )(x, g.reshape(1, d), mod)
    n_e = w_router.shape[1]
    assert n_e <= LANES
    w_router = jnp.pad(w_router, ((0, 0), (0, LANES - n_e)))
    return pl.pallas_call(
        functools.partial(_norm_mod_router_kernel, ci=ci, n_e=n_e),
        out_shape=(jax.ShapeDtypeStruct((rows, d), F32),
                   jax.ShapeDtypeStruct((rows, ROUTE_COLS), F32),
                   jax.ShapeDtypeStruct((1, LANES), F32)),
        grid=(rows // tr,),
        in_specs=in_specs + [pl.BlockSpec((d, LANES), lambda i: (0, 0))],
        out_specs=(h_spec, pl.BlockSpec((tr, ROUTE_COLS), lambda i: (i, 0)),
                   pl.BlockSpec((1, LANES), lambda i: (0, 0))),
        scratch_shapes=[pltpu.VMEM((1, LANES), F32)],
        compiler_params=_cparams(1), name="norm_modulate_router",
    )(x, g.reshape(1, d), mod, w_router)


def _final_norm_kernel(x_ref, g_ref, o_ref):
    x = x_ref[...]
    o_ref[...] = x * lax.rsqrt(jnp.mean(x * x, axis=-1, keepdims=True) + NORM_EPS) * g_ref[...]


def _final_norm(dm, x, g):
    rows, d = x.shape
    tr = dm.tr
    return pl.pallas_call(
        _final_norm_kernel,
        out_shape=jax.ShapeDtypeStruct((rows, d), F32),
        grid=(rows // tr,),
        in_specs=[pl.BlockSpec((tr, d), lambda i: (i, 0)), pl.BlockSpec((1, d), lambda i: (0, 0))],
        out_specs=pl.BlockSpec((tr, d), lambda i: (i, 0)),
        compiler_params=_cparams(1), name="final_norm",
    )(x, g.reshape(1, d))


def _rope_epilogue(acc, cos_ref, sina_ref, sinb_ref, g_ref, o_ref, scale):
    cos, sina, sinb = cos_ref[...], sina_ref[...], sinb_ref[...]
    for c in range(acc.shape[1] // HEAD_DIM):
        xh = acc[:, c * HEAD_DIM:(c + 1) * HEAD_DIM]
        if g_ref is not None:
            xh = xh * lax.rsqrt(jnp.mean(xh * xh, axis=-1, keepdims=True) + NORM_EPS) * g_ref[...]
        fwd = pltpu.roll(xh, HEAD_DIM - ROPE_FREQS, 1)
        bwd = pltpu.roll(xh, ROPE_FREQS, 1)
        r = xh * cos + fwd * sina + bwd * sinb
        if scale != 1.0:
            r = r * scale
        o_ref[:, c * HEAD_DIM:(c + 1) * HEAD_DIM] = r.astype(o_ref.dtype)


def _mm_kernel(*refs, mode, n_a, lat_tiles, tail_rows, n_w, cast, n_extra, gate_idx, scale):
    a_refs = refs[:n_a]
    w_refs = refs[n_a:n_a + n_w]
    extra = refs[n_a + n_w:n_a + n_w + n_extra]
    o_ref = refs[n_a + n_w + n_extra]
    wb_refs = refs[n_a + n_w + n_extra + 1:]

    if cast:
        @pl.when(pl.program_id(1) == 0)
        def _():
            for w_ref, wb_ref in zip(w_refs, wb_refs):
                wb_ref[...] = w_ref[...].astype(BF16)
        w_srcs = wb_refs
    else:
        w_srcs = w_refs

    tm = o_ref.shape[0]
    if n_a == 1 and tail_rows == tm:
        _mm_tile(a_refs[0], tm, w_srcs, extra, o_ref, mode, gate_idx, scale)
        return

    i = pl.program_id(1)

    @pl.when(i < lat_tiles)
    def _():
        _mm_tile(a_refs[0], tm, w_srcs, extra, o_ref, mode, gate_idx, scale)

    @pl.when(i >= lat_tiles)
    def _():
        _mm_tile(a_refs[-1], tail_rows, w_srcs, extra, o_ref, mode, gate_idx, scale)


def _mm_tile(a_ref, nrows, w_srcs, extra, o_ref, mode, gate_idx, scale):
    rows = lambda ref: ref.at[0:nrows]
    a = a_ref[0:nrows, :]
    accs = [jnp.dot(a, w[...], preferred_element_type=F32) for w in w_srcs]
    n_extra = len(extra)
    o_ref = rows(o_ref)

    if mode == "plain":
        o_ref[...] = accs[0].astype(o_ref.dtype)
    elif mode == "rope":
        g_ref = extra[3] if n_extra == 4 else None
        _rope_epilogue(accs[0], rows(extra[0]), rows(extra[1]), rows(extra[2]), g_ref, o_ref, scale)
    elif mode == "resid":
        x_ref, mod_ref = extra
        gate = mod_ref[gate_idx:gate_idx + 1, :]
        o_ref[...] = rows(x_ref)[...] + gate * accs[0]
    elif mode == "swiglu":
        o_ref[...] = (_silu(accs[0]) * accs[1]).astype(o_ref.dtype)
    else:
        raise ValueError(mode)


def _pick_tn(k, n, w_itemsize, n_w, tm, io_bytes):
    best = LANES if n % LANES == 0 else n
    for tn in range(LANES, n + 1, LANES):
        w_blocks = n_w * k * tn * w_itemsize
        need = 2 * tm * k * 2 + 2 * w_blocks + (n_w * k * tn * 2 if w_itemsize == 4 else 0) + 2 * tm * tn * io_bytes
        if n % tn == 0 and w_blocks <= W_TILE_BYTES and need <= VMEM_TILE_BUDGET:
            best = tn
    return best


def _matmul(dm, a, w, *, rows, mode, out_dtype, e0=0, n_e=1, col0=0, n_cols=None, tn=None,
            extra=(), extra_specs=(), gate_idx=0, scale=1.0, swiglu_half=None, a_groups=False):
    a_parts = list(a) if isinstance(a, (list, tuple)) else [a]
    _, k, n_w_cols = w.shape
    assert all(p.shape[1] == (n_e * k if a_groups else k) for p in a_parts)
    cast = w.dtype != BF16
    n_w = 2 if mode == "swiglu" else 1
    if n_cols is None:
        n_cols = swiglu_half if mode == "swiglu" else n_w_cols - col0
    tm = dm.tm
    if tn is None:
        io_bytes = jnp.dtype(out_dtype).itemsize + (4 if mode == "resid" else 0)
        tn = _pick_tn(k, n_cols, w.dtype.itemsize, n_w, tm, io_bytes)
    assert n_cols % tn == 0 and col0 % tn == 0
    bpe = n_cols // tn
    blk0 = col0 // tn

    def w_map(off):
        if a_groups:
            return lambda j, i: (e0, 0, blk0 + off + j % bpe)
        return lambda j, i: (e0 + j // bpe, 0, blk0 + off + j % bpe)

    if len(a_parts) == 1:
        lat_tiles, tail_rows = divmod(rows, tm)
        row_tiles = lat_tiles + (1 if tail_rows else 0)
        tail_rows = tail_rows or tm
        a_specs = [pl.BlockSpec((tm, k), (lambda j, i: (i, j // bpe)) if a_groups else (lambda j, i: (i, 0)))]
    else:
        lat_rows, ctx_rows = a_parts[0].shape[0], a_parts[1].shape[0]
        tail_rows = min(tm, ctx_rows)
        assert not a_groups and lat_rows % tm == 0 and ctx_rows % tail_rows == 0
        assert rows == lat_rows + ctx_rows
        lat_tiles = lat_rows // tm
        row_tiles = lat_tiles + ctx_rows // tail_rows
        a_specs = [pl.BlockSpec((tm, k), lambda j, i: (jnp.minimum(i, lat_tiles - 1), 0)),
                   pl.BlockSpec((tail_rows, k), lambda j, i: (jnp.maximum(i - lat_tiles, 0), 0))]
    in_specs = a_specs + [pl.BlockSpec((None, k, tn), w_map(0))]
    args = a_parts + [w]
    if n_w == 2:
        assert swiglu_half % tn == 0
        in_specs.append(pl.BlockSpec((None, k, tn), w_map(swiglu_half // tn)))
        args.append(w)
    in_specs += list(extra_specs)
    args += list(extra)
    scratch = [pltpu.VMEM((k, tn), BF16) for _ in range(n_w)] if cast else []
    kern = functools.partial(_mm_kernel, mode=mode, n_a=len(a_parts), lat_tiles=lat_tiles,
                             tail_rows=tail_rows, n_w=n_w, cast=cast, n_extra=len(extra),
                             gate_idx=gate_idx, scale=scale)
    return pl.pallas_call(
        kern,
        out_shape=jax.ShapeDtypeStruct((rows, n_e * n_cols), out_dtype),
        grid=(n_e * bpe, row_tiles),
        in_specs=in_specs,
        out_specs=pl.BlockSpec((tm, tn), lambda j, i: (i, j)),
        scratch_shapes=scratch,
        compiler_params=_cparams(2),
        name="matmul_" + mode,
    )(*args)


def _proj_rope(dm, h, w, layer, col0, n_cols, rows, tabs, scale=1.0, gain=None):
    tm = dm.tm
    tab_spec = pl.BlockSpec((tm, HEAD_DIM), lambda j, i: (i, 0))
    extra, specs = list(tabs), [tab_spec] * 3
    if gain is not None:
        extra.append(gain.reshape(1, HEAD_DIM))
        specs.append(pl.BlockSpec((1, HEAD_DIM), lambda j, i: (0, 0)))
    return _matmul(dm, h, w, rows=rows, mode="rope", out_dtype=BF16, e0=layer, col0=col0,
                   n_cols=n_cols, extra=extra, extra_specs=specs, scale=scale)


def _proj_plain(dm, h, w, layer, col0, n_cols, rows):
    return _matmul(dm, h, w, rows=rows, mode="plain", out_dtype=BF16, e0=layer, col0=col0,
                   n_cols=n_cols)


def _proj_resid(dm, a, w, layer, x, mod, gate_idx, rows):
    tm = dm.tm
    _, k, n = w.shape
    tn = _pick_tn(k, n, w.dtype.itemsize, 1, tm, 8)
    specs = [pl.BlockSpec((tm, tn), lambda j, i: (i, j)),
             pl.BlockSpec((None, N_MOD, tn), lambda j, i: (dm.group(i, tm), 0, j))]
    return _matmul(dm, a, w, rows=rows, mode="resid", out_dtype=F32, e0=layer, tn=tn,
                   extra=[x, mod], extra_specs=specs, gate_idx=gate_idx)


def _route_tables(route, counts, n_e, rows):
    tg = MOE_TILE
    expert = route[:, 0:2].astype(jnp.int32)
    rank = route[:, 2:4].astype(jnp.int32)
    cnt = counts[0, :n_e].astype(jnp.int32)
    padded = (cnt + tg - 1) // tg * tg
    ends = jnp.cumsum(padded)
    slots = (ends - padded)[expert] + rank
    assert (2 * rows) % tg == 0
    n_tiles = 2 * rows // tg + n_e
    tok = jnp.broadcast_to(jnp.arange(rows, dtype=jnp.int32)[:, None], (rows, 2))
    src = jnp.zeros((n_tiles * tg,), jnp.int32).at[slots.reshape(-1)].set(tok.reshape(-1))
    tile_row0 = jnp.arange(n_tiles, dtype=jnp.int32)[:, None] * tg
    tile_expert = jnp.minimum(jnp.sum(tile_row0 >= ends[None, :], axis=1), n_e - 1).astype(jnp.int32)
    return slots, src, tile_expert, n_tiles


def _rows_copy(src_hbm, dst, sem, n):
    return pltpu.make_async_copy(src_hbm.at[pl.ds(0, n), :], dst, sem)


def _gather_rows(idx_ref, stride, offset, src_hbm, dst_ref, sem):
    def issue(r, carry):
        row = idx_ref[0, stride * r + offset]
        pltpu.make_async_copy(src_hbm.at[pl.ds(row, 1), :], dst_ref.at[pl.ds(r, 1), :], sem).start()
        return carry

    lax.fori_loop(0, dst_ref.shape[0], issue, 0, unroll=GATHER_UNROLL)


def _dispatch_kernel(cur_ref, nxt_ref, z_hbm, o_ref, buf_ref, sem):
    g = pl.program_id(0)
    slot = g % 2

    @pl.when(g == 0)
    def _():
        _gather_rows(cur_ref, 1, 0, z_hbm, buf_ref.at[0], sem.at[0])

    @pl.when(g + 1 < pl.num_programs(0))
    def _():
        _gather_rows(nxt_ref, 1, 0, z_hbm, buf_ref.at[1 - slot], sem.at[1 - slot])

    _rows_copy(z_hbm, buf_ref.at[slot], sem.at[slot], buf_ref.shape[1]).wait()
    o_ref[...] = buf_ref[slot].astype(o_ref.dtype)


def _dispatch(z, src, n_tiles):
    tg = MOE_TILE
    d = z.shape[1]
    src = src.reshape(n_tiles, 1, tg)
    return pl.pallas_call(
        _dispatch_kernel,
        out_shape=jax.ShapeDtypeStruct((n_tiles * tg, d), BF16),
        grid=(n_tiles,),
        in_specs=[pl.BlockSpec((None, 1, tg), lambda g: (g, 0, 0), memory_space=pltpu.SMEM),
                  pl.BlockSpec((None, 1, tg), lambda g: (jnp.minimum(g + 1, n_tiles - 1), 0, 0),
                               memory_space=pltpu.SMEM),
                  pl.BlockSpec(memory_space=pl.ANY)],
        out_specs=pl.BlockSpec((tg, d), lambda g: (g, 0)),
        scratch_shapes=[pltpu.VMEM((2, tg, d), z.dtype), pltpu.SemaphoreType.DMA((2,))],
        compiler_params=_cparams(1), name="moe_dispatch",
    )(src, src, z)


def _expert_changed(te_ref):
    g = pl.program_id(1)
    return (g == 0) | (te_ref[g] != te_ref[jnp.maximum(g - 1, 0)])


def _moe_up_kernel(te_ref, a_ref, wg_ref, wu_ref, o_ref, wgb_ref, wub_ref):
    @pl.when(_expert_changed(te_ref))
    def _():
        wgb_ref[...] = wg_ref[...].astype(BF16)
        wub_ref[...] = wu_ref[...].astype(BF16)

    a = a_ref[...]
    hg = jnp.dot(a, wgb_ref[...], preferred_element_type=F32)
    hu = jnp.dot(a, wub_ref[...], preferred_element_type=F32)
    o_ref[...] = (_silu(hg) * hu).astype(o_ref.dtype)


def _moe_down_kernel(te_ref, a_ref, w_ref, o_ref, wb_ref):
    @pl.when(_expert_changed(te_ref))
    def _():
        wb_ref[...] = w_ref[...].astype(BF16)

    o_ref[...] = jnp.dot(a_ref[...], wb_ref[...], preferred_element_type=F32)


def _moe_up(zg, w_in, e0, tile_expert, f):
    tg = MOE_TILE
    s, d = zg.shape
    tn = min(f, MOE_UP_COLS)
    assert f % tn == 0
    nb = f // tn
    grid_spec = pltpu.PrefetchScalarGridSpec(
        num_scalar_prefetch=1, grid=(nb, s // tg),
        in_specs=[pl.BlockSpec((tg, d), lambda j, g, te: (g, 0)),
                  pl.BlockSpec((None, d, tn), lambda j, g, te: (e0 + te[g], 0, j)),
                  pl.BlockSpec((None, d, tn), lambda j, g, te: (e0 + te[g], 0, nb + j))],
        out_specs=pl.BlockSpec((tg, tn), lambda j, g, te: (g, j)),
        scratch_shapes=[pltpu.VMEM((d, tn), BF16), pltpu.VMEM((d, tn), BF16)])
    return pl.pallas_call(
        _moe_up_kernel, out_shape=jax.ShapeDtypeStruct((s, f), BF16), grid_spec=grid_spec,
        compiler_params=_cparams(2), name="moe_up",
    )(tile_expert, zg, w_in, w_in)


def _moe_down(hdn, w_out, e0, tile_expert):
    tg = MOE_TILE
    s, f = hdn.shape
    d = w_out.shape[2]
    tn = _pick_tn(f, d, 4, 1, tg, 4)
    grid_spec = pltpu.PrefetchScalarGridSpec(
        num_scalar_prefetch=1, grid=(d // tn, s // tg),
        in_specs=[pl.BlockSpec((tg, f), lambda j, g, te: (g, 0)),
                  pl.BlockSpec((None, f, tn), lambda j, g, te: (e0 + te[g], 0, j))],
        out_specs=pl.BlockSpec((tg, tn), lambda j, g, te: (g, j)),
        scratch_shapes=[pltpu.VMEM((f, tn), BF16)])
    return pl.pallas_call(
        _moe_down_kernel, out_shape=jax.ShapeDtypeStruct((s, d), F32), grid_spec=grid_spec,
        compiler_params=_cparams(2), name="moe_down",
    )(tile_expert, hdn, w_out)


def _combine_kernel(cur_ref, nxt_ref, y_hbm, x_ref, route_ref, mod_ref, o_ref, y_ref, sem, *, gate_idx):
    i = pl.program_id(0)
    slot = i % 2
    tr = x_ref.shape[0]

    def start(idx_ref, s):
        for c in range(2):
            _gather_rows(idx_ref, 2, c, y_hbm, y_ref.at[s, c], sem.at[s, c])

    @pl.when(i == 0)
    def _():
        start(cur_ref, 0)

    @pl.when(i + 1 < pl.num_programs(0))
    def _():
        start(nxt_ref, 1 - slot)

    for c in range(2):
        _rows_copy(y_hbm, y_ref.at[slot, c], sem.at[slot, c], tr).wait()
    route = route_ref[...]
    gate = mod_ref[gate_idx:gate_idx + 1, :]
    o_ref[...] = x_ref[...] + gate * (route[:, 4:5] * y_ref[slot, 0] + route[:, 5:6] * y_ref[slot, 1])


def _moe_combine(dm, yg, slots, x, route, mod, gate_idx, rows):
    tr = COMBINE_ROW_TILE
    d = x.shape[1]
    n = rows // tr
    slots = slots.reshape(n, 1, 2 * tr)
    return pl.pallas_call(
        functools.partial(_combine_kernel, gate_idx=gate_idx),
        out_shape=jax.ShapeDtypeStruct((rows, d), F32),
        grid=(n,),
        in_specs=[pl.BlockSpec((None, 1, 2 * tr), lambda i: (i, 0, 0), memory_space=pltpu.SMEM),
                  pl.BlockSpec((None, 1, 2 * tr), lambda i: (jnp.minimum(i + 1, n - 1), 0, 0),
                               memory_space=pltpu.SMEM),
                  pl.BlockSpec(memory_space=pl.ANY),
                  pl.BlockSpec((tr, d), lambda i: (i, 0)),
                  pl.BlockSpec((tr, ROUTE_COLS), lambda i: (i, 0)),
                  pl.BlockSpec((None, N_MOD, d), lambda i: (dm.group(i, tr), 0, 0))],
        out_specs=pl.BlockSpec((tr, d), lambda i: (i, 0)),
        scratch_shapes=[pltpu.VMEM((2, 2, tr, d), F32), pltpu.SemaphoreType.DMA((2, 2))],
        compiler_params=_cparams(1), name="moe_combine",
    )(slots, slots, yg, x, route, mod)


_NT = (((1,), (1,)), ((), ()))


def _key_chunks(refs):
    out, off = [], 0
    for r in refs:
        n = r.shape[0]
        ck = min(KEY_CHUNK, n)
        for r0 in range(0, n, ck):
            out.append((r, r0, ck, off))
            off += ck
    return out


def _lane_fold(acc, v, op):
    for t in range(v.shape[1] // LANES):
        piece = v[:, t * LANES:(t + 1) * LANES]
        acc = piece if acc is None else op(acc, piece)
    return acc


def _scores_pass(q, k_refs, lo, hi, s_ref):
    mx = None
    for r, r0, ck, off in _key_chunks(k_refs):
        s = lax.dot_general(q, r[r0:r0 + ck, lo:hi], _NT, preferred_element_type=F32)
        s_ref[:, off:off + ck] = s
        mx = _lane_fold(mx, s, jnp.maximum)
    return jnp.max(mx, axis=-1, keepdims=True)


def _values_pass(s_ref, m, v_refs, lo, hi):
    acc, ls = None, None
    for r, r0, ck, off in _key_chunks(v_refs):
        e = jnp.exp2(s_ref[:, off:off + ck] - m)
        ls = _lane_fold(ls, e, jnp.add)
        pv = jnp.dot(e.astype(BF16), r[r0:r0 + ck, lo:hi], preferred_element_type=F32)
        acc = pv if acc is None else acc + pv
    return acc * (1.0 / jnp.sum(ls, axis=-1, keepdims=True))


def _diff_attn_body(q_ref, k_refs, v_refs, lam_ref, g_ref, o_ref, s_refs, e_refs, sub, lam_init):
    lv = lam_ref[...]
    lam = (jnp.exp(jnp.sum(lv[0:1] * lv[1:2], axis=-1, keepdims=True))
           - jnp.exp(jnp.sum(lv[2:3] * lv[3:4], axis=-1, keepdims=True)) + lam_init)
    n_sub = q_ref.shape[0] // sub
    chains = [(t, c) for t in range(n_sub) for c in range(2)]
    m = {}
    for t, c in chains:
        lo, hi = c * HEAD_DIM, (c + 1) * HEAD_DIM
        m[t, c] = _scores_pass(q_ref[t * sub:(t + 1) * sub, lo:hi], k_refs, lo, hi, s_refs[2 * t + c])
    r = {}
    for t, c in chains:
        ls = None
        s_ref, e_ref = s_refs[2 * t + c], e_refs[2 * t + c]
        for _, _, ck, off in _key_chunks(k_refs):
            e = jnp.exp2(s_ref[:, off:off + ck] - m[t, c])
            ls = _lane_fold(ls, e, jnp.add)
            e_ref[:, off:off + ck] = e.astype(BF16)
        r[t, c] = 1.0 / jnp.sum(ls, axis=-1, keepdims=True)
    for t in range(n_sub):
        w0 = r[t, 0].astype(BF16)
        w1 = (lam * r[t, 1]).astype(BF16)
        o = None
        for vr, r0_, ck, off in _key_chunks(v_refs):
            a = e_refs[2 * t][:, off:off + ck] * w0 - e_refs[2 * t + 1][:, off:off + ck] * w1
            pv = jnp.dot(a, vr[r0_:r0_ + ck, :], preferred_element_type=F32)
            o = pv if o is None else o + pv
        o = o * lax.rsqrt(jnp.mean(o * o, axis=-1, keepdims=True) + NORM_EPS) * g_ref[...]
        o_ref[t * sub:(t + 1) * sub, :] = (o * (1.0 - lam_init)).astype(o_ref.dtype)


def _gqa_attn_body(q_ref, k_refs, v_refs, o_ref, s_refs):
    m = [_scores_pass(q_ref[:, g * HEAD_DIM:(g + 1) * HEAD_DIM], k_refs, 0, HEAD_DIM, s_refs[g])
         for g in range(GQA_GROUP)]
    for g in range(GQA_GROUP):
        o = _values_pass(s_refs[g], m[g], v_refs, 0, HEAD_DIM)
        o_ref[:, g * HEAD_DIM:(g + 1) * HEAD_DIM] = o.astype(o_ref.dtype)


def _attn_kernel(q_ref, *rest, kind, n_src, n_chains, sub, lam_init):
    k_refs = rest[:n_src]
    v_refs = rest[n_src:2 * n_src]
    n_scratch = 2 * n_chains if kind == "diff" else n_chains
    params = rest[2 * n_src:-n_scratch - 1]
    o_ref = rest[-n_scratch - 1]
    s_refs = rest[-n_scratch:][:n_chains]
    if kind == "diff":
        e_refs = rest[-n_chains:]
        _diff_attn_body(q_ref, k_refs, v_refs, params[0], params[1], o_ref, s_refs, e_refs, sub, lam_init)
    else:
        _gqa_attn_body(q_ref, k_refs, v_refs, o_ref, s_refs)


def _attention_call(dm, q, k, v, *, kind, latent, params, lam_init):
    sub = dm.tq
    n_sub = DIFF_SUBTILES if (kind == "diff" and latent and dm.seq % (DIFF_SUBTILES * sub) == 0) else 1
    tq = sub * n_sub
    wq = 2 * HEAD_DIM if kind == "diff" else GQA_GROUP * HEAD_DIM
    wk = 2 * HEAD_DIM if kind == "diff" else HEAD_DIM
    n_heads = k.shape[1] // wk
    lat_per_ctx = dm.m_lat // dm.ctx
    lat_spec = pl.BlockSpec((dm.seq, wk), lambda b, h, qi: (b, h))
    ctx_spec = pl.BlockSpec((dm.ctx, wk), lambda b, h, qi: (lat_per_ctx + b, h))
    if latent:
        n_q, row0, srcs, n_keys = dm.seq // tq, 0, [lat_spec, ctx_spec], dm.seq + dm.ctx
    else:
        n_q, row0, srcs, n_keys = dm.ctx // tq, dm.m_lat // tq, [ctx_spec], dm.ctx
    q_spec = pl.BlockSpec((tq, wq), lambda b, h, qi: (row0 + b * n_q + qi, h))
    in_specs = [q_spec] + srcs + srcs + [pl.BlockSpec(p.shape, lambda b, h, qi: (0, 0)) for p in params]
    args = [q] + [k] * len(srcs) + [v] * len(srcs) + list(params)
    n_chains = (2 if kind == "diff" else GQA_GROUP) * n_sub
    return pl.pallas_call(
        functools.partial(_attn_kernel, kind=kind, n_src=len(srcs), n_chains=n_chains, sub=sub,
                          lam_init=lam_init),
        out_shape=jax.ShapeDtypeStruct((dm.b * n_q * tq, q.shape[1]), BF16),
        grid=(dm.b, n_heads, n_q),
        in_specs=in_specs,
        out_specs=pl.BlockSpec((tq, wq), lambda b, h, qi: (b * n_q + qi, h)),
        scratch_shapes=([pltpu.VMEM((sub, n_keys), F32) for _ in range(n_chains)]
                        + [pltpu.VMEM((sub, n_keys), BF16) for _ in range(n_chains if kind == "diff" else 0)]),
        compiler_params=_cparams(3),
        name=kind + ("_attention" if latent else "_attention_ctx"),
    )(*args)


def _attention(dm, q, k, v, *, kind, ctx_out, params=(), lam_init=0.0):
    common = dict(kind=kind, params=params, lam_init=lam_init)
    y = [_attention_call(dm, q, k, v, latent=True, **common)]
    if ctx_out:
        y.append(_attention_call(dm, q, k, v, latent=False, **common))
    return y


def _dft_tables(n, norm):
    n0 = 1
    while n0 * n0 * 4 <= n and n % (n0 * 2) == 0:
        n0 *= 2
    n1 = n // n0
    k = jnp.arange(n, dtype=jnp.int32)[:, None]

    def cs(m):
        ang = ((k * m) % n).astype(F32) * (2.0 * math.pi / n)
        return jnp.cos(ang), jnp.sin(ang)

    c1, s1 = cs(jnp.arange(n1, dtype=jnp.int32)[None, :] * n0)
    c0, s0 = cs(jnp.arange(n0, dtype=jnp.int32)[None, :])
    c1, s1, c0, s0 = c1[:, :, None], s1[:, :, None], c0[:, None, :], s0[:, None, :]
    cos = (c1 * c0 - s1 * s0).reshape(n, n) * norm
    sin = (s1 * c0 + c1 * s0).reshape(n, n) * norm
    return cos, sin


def _dft_rows_kernel(c_ref, s_ref, yc_ref, ys_ref, o_ref):
    o_ref[...] = (jnp.dot(c_ref[...], yc_ref[...], preferred_element_type=F32)
                  + jnp.dot(s_ref[...], ys_ref[...], preferred_element_type=F32)).astype(o_ref.dtype)


def _dft_rows(dm, y, n, row0):
    d = dm.d
    dg = d // FOURIER_GROUPS
    cos, sin = _dft_tables(n, n ** -0.5)
    cos, nsin = cos.astype(BF16), (-sin).astype(BF16)
    tmf = min(DFT_ROW_TILE, n)
    tn = min(512, dg)
    lb = dg // tn
    rb0 = row0 // n

    def y_map(off):
        return lambda b, j, i: (rb0 + b, (j // lb) * 2 * lb + off + j % lb)

    in_specs = [pl.BlockSpec((tmf, n), lambda b, j, i: (i, 0)),
                pl.BlockSpec((tmf, n), lambda b, j, i: (i, 0)),
                pl.BlockSpec((n, tn), y_map(0)),
                pl.BlockSpec((n, tn), y_map(lb))]
    tiles = n // tmf
    return pl.pallas_call(
        _dft_rows_kernel,
        out_shape=jax.ShapeDtypeStruct((dm.b * n, d), BF16),
        grid=(dm.b, d // tn, tiles),
        in_specs=in_specs,
        out_specs=pl.BlockSpec((tmf, tn), lambda b, j, i: (b * tiles + i, j)),
        compiler_params=_cparams(3),
        name="dft_positions",
    )(cos, nsin, y, y)


def _fourier_mix(dm, h, ctx_out):
    d = dm.d
    dg = d // FOURIER_GROUPS
    rows = h.shape[0]
    cos_c, sin_c = _dft_tables(dg, dg ** -0.5)
    cs = jnp.concatenate([cos_c, sin_c], axis=1).astype(BF16)[None]
    y = _matmul(dm, h, cs, rows=rows, mode="plain", out_dtype=BF16, n_e=FOURIER_GROUPS,
                a_groups=True, tn=min(2 * dg, 1024))
    f = [_dft_rows(dm, y, dm.seq, 0)]
    if ctx_out:
        f.append(_dft_rows(dm, y, dm.ctx, dm.m_lat))
    return f


def _rope_tables(dm):
    rows = dm.seq // GRID_W
    r, col = jnp.meshgrid(jnp.arange(rows), jnp.arange(GRID_W), indexing="ij")
    pos = jnp.stack([r.reshape(-1), col.reshape(-1)], axis=-1).astype(F32)
    inv_freq = 1.0 / (ROPE_THETA ** (jnp.arange(ROPE_FREQS, dtype=F32) / ROPE_FREQS))
    ang = pos[:, :, None] * inv_freq
    cos, sin = jnp.cos(ang), jnp.sin(ang)
    zero = jnp.zeros_like(sin)
    cos_t = jnp.stack([cos, cos], axis=2).reshape(dm.seq, HEAD_DIM)
    sina_t = jnp.stack([-sin, zero], axis=2).reshape(dm.seq, HEAD_DIM)
    sinb_t = jnp.stack([zero, sin], axis=2).reshape(dm.seq, HEAD_DIM)

    def full(t, fill):
        return jnp.concatenate([jnp.tile(t, (dm.b, 1)), jnp.full((dm.m_ctx, HEAD_DIM), fill, F32)], axis=0)

    return full(cos_t, 1.0), full(sina_t, 0.0), full(sinb_t, 0.0)


def kernel(x, c, ctx, c_ctx, w_mod, b_mod, norm_g, diff_w_in, diff_w_out, diff_lambda, diff_subln_g,
           fourier_w_out, gqa_w_in, gqa_w_out, gqa_qk_g, ffn_w_in, ffn_w_out, moe_router, moe_w_in,
           moe_w_out, final_g):
    dm = _Dims(x, ctx)
    b, d = dm.b, dm.d
    depth = w_mod.shape[0]
    q_scale = HEAD_DIM ** -0.5 * LOG2E

    cc = jnp.concatenate([c, c_ctx[None, :], jnp.zeros((8 - b - 1, d), F32)], axis=0)
    mods = _modulations(cc, w_mod, b_mod)[:, :b + 1].reshape(depth, b + 1, N_MOD, d)
    tabs = _rope_tables(dm)
    xs = jnp.concatenate([x.reshape(dm.m_lat, d), ctx.reshape(dm.m_ctx, d)], axis=0)

    n_moe, n_e, _, two_f = moe_w_in.shape
    moe_w_in = moe_w_in.reshape(n_moe * n_e, d, two_f)
    moe_w_out = moe_w_out.reshape(n_moe * n_e, two_f // 2, d)

    for i in range(depth):
        last = i == depth - 1
        mod = mods[i]
        rows_in = xs.shape[0]
        rows_out = dm.m_lat if last else dm.m_all
        h = _norm_mod(dm, xs, norm_g[i, 0], mod, 0, rows_in)
        kind, j = i % N_MIXERS, i // N_MIXERS
        if kind == 0:
            qk = diff_w_in.shape[2] // 3
            q = _proj_rope(dm, h, diff_w_in, j, 0, qk, rows_out, tabs, scale=q_scale)
            k = _proj_rope(dm, h, diff_w_in, j, qk, qk, rows_in, tabs)
            v = _proj_plain(dm, h, diff_w_in, j, 2 * qk, qk, rows_in)
            lam_init = 0.8 - 0.6 * math.exp(-0.3 * i)
            y = _attention(dm, q, k, v, kind="diff", ctx_out=not last,
                           params=(diff_lambda[j], diff_subln_g[j].reshape(1, 2 * HEAD_DIM)),
                           lam_init=lam_init)
            w_out = diff_w_out
        elif kind == 1:
            y = _fourier_mix(dm, h, not last)
            w_out = fourier_w_out
        else:
            kvd = (gqa_w_in.shape[2] - d) // 2
            q = _proj_rope(dm, h, gqa_w_in, j, 0, d, rows_out, tabs, scale=q_scale, gain=gqa_qk_g[j, 0])
            k = _proj_rope(dm, h, gqa_w_in, j, d, kvd, rows_in, tabs, gain=gqa_qk_g[j, 1])
            v = _proj_plain(dm, h, gqa_w_in, j, d + kvd, kvd, rows_in)
            y = _attention(dm, q, k, v, kind="gqa", ctx_out=not last)
            w_out = gqa_w_out
        xs = _proj_resid(dm, y, w_out, j, xs, mod, 2, rows_out)

        f = i // 2
        if i % 2 == 0:
            z = _norm_mod(dm, xs, norm_g[i, 1], mod, 3, rows_out)
            hdn = _matmul(dm, z, ffn_w_in, rows=rows_out, mode="swiglu", out_dtype=BF16, e0=f,
                          swiglu_half=ffn_w_in.shape[2] // 2)
            xs = _proj_resid(dm, hdn, ffn_w_out, f, xs, mod, 5, rows_out)
        else:
            z, route, counts = _norm_mod(dm, xs, norm_g[i, 1], mod, 3, rows_out, w_router=moe_router[f])
            slots, src, tile_expert, n_tiles = _route_tables(route, counts, n_e, rows_out)
            zg = _dispatch(z, src, n_tiles)
            hdn = _moe_up(zg, moe_w_in, f * n_e, tile_expert, two_f // 2)
            yg = _moe_down(hdn, moe_w_out, f * n_e, tile_expert)
            xs = _moe_combine(dm, yg, slots, xs, route, mod, 5, rows_out)

    return _final_norm(dm, xs, final_g).reshape(b, dm.seq, d)
```

```python
import functools
import math

import jax
import jax.numpy as jnp
from jax import lax
from jax.experimental import pallas as pl
from jax.experimental.pallas import tpu as pltpu

HEAD_DIM = 128
GRID_W = 64
ROPE_THETA = 10000.0
ROPE_FREQS = HEAD_DIM // 4
NORM_EPS = 1e-6
N_MOD = 6
N_MIXERS = 3
FOURIER_GROUPS = 4
GQA_GROUP = 4

LANES = 128
VMEM_LIMIT_BYTES = 56 * 2**20
ROW_TILE = 256
DFT_ROW_TILE = 512
VMEM_TILE_BUDGET = 48 * 2**20
NORM_ROW_TILE = 256
MOE_TILE = 256
MOE_UP_COLS = 512
COMBINE_ROW_TILE = 128
GATHER_UNROLL = 8
Q_TILE = 256
DIFF_SUBTILES = 2
KEY_CHUNK = 512
LOG2E = math.log2(math.e)
W_TILE_BYTES = 16 * 2**20

F32 = jnp.float32
BF16 = jnp.bfloat16


def _cparams(n_axes):
    return pltpu.CompilerParams(dimension_semantics=("arbitrary",) * n_axes,
                                vmem_limit_bytes=VMEM_LIMIT_BYTES)


def _silu(v):
    return v / (1.0 + jnp.exp(-v))


class _Dims:
    def __init__(self, x, ctx):
        self.b, self.seq, self.d = x.shape
        self.ctx = ctx.shape[1]
        self.m_lat = self.b * self.seq
        self.m_ctx = self.b * self.ctx
        self.m_all = self.m_lat + self.m_ctx
        g = math.gcd(self.seq, self.m_ctx)
        self.tm = math.gcd(ROW_TILE, self.seq)
        self.tr = min(NORM_ROW_TILE, g)
        self.tq = min(Q_TILE, self.ctx)
        assert self.seq % self.tq == 0 and self.ctx % self.tq == 0

    def group(self, i, tile):
        r = i * tile
        return jnp.where(r < self.m_lat, r // self.seq, self.b)


def _mod_kernel(c_ref, w_ref, b_ref, o_ref):
    s = _silu(c_ref[...]).astype(BF16)
    w = w_ref[...].astype(BF16)
    o_ref[...] = jnp.dot(s, w, preferred_element_type=F32) + b_ref[...]


def _modulations(cc, w_mod, b_mod):
    depth, d, n = w_mod.shape
    tn = 512 if n % 512 == 0 else n
    rows = cc.shape[0]
    return pl.pallas_call(
        _mod_kernel,
        out_shape=jax.ShapeDtypeStruct((depth, rows, n), F32),
        grid=(depth, n // tn),
        in_specs=[pl.BlockSpec((rows, d), lambda l, j: (0, 0)),
                  pl.BlockSpec((None, d, tn), lambda l, j: (l, 0, j)),
                  pl.BlockSpec((None, 1, tn), lambda l, j: (l, 0, j))],
        out_specs=pl.BlockSpec((None, rows, tn), lambda l, j: (l, 0, j)),
        compiler_params=_cparams(2),
        name="adaln_modulations",
    )(cc, w_mod, b_mod.reshape(depth, 1, n))


def _norm_mod_value(x_ref, g_ref, mod_ref, ci):
    x = x_ref[...]
    y = x * lax.rsqrt(jnp.mean(x * x, axis=-1, keepdims=True) + NORM_EPS) * g_ref[...]
    shift = mod_ref[ci:ci + 1, :]
    scale = mod_ref[ci + 1:ci + 2, :]
    return y * (1.0 + scale) + shift


def _norm_mod_kernel(x_ref, g_ref, mod_ref, o_ref, *, ci):
    o_ref[...] = _norm_mod_value(x_ref, g_ref, mod_ref, ci).astype(BF16)


def _split_bf16(v):
    hi = v.astype(BF16)
    lo = (v - hi.astype(F32)).astype(BF16)
    return hi, lo


ROUTE_COLS = 8


def _norm_mod_router_kernel(x_ref, g_ref, mod_ref, wr_ref, o_ref, route_ref, counts_ref, run_ref, *, ci, n_e):
    z = _norm_mod_value(x_ref, g_ref, mod_ref, ci)
    o_ref[...] = z
    z_hi, z_lo = _split_bf16(z)
    w_hi, w_lo = _split_bf16(wr_ref[...])
    logits = (jnp.dot(z_hi, w_hi, preferred_element_type=F32)
              + jnp.dot(z_lo, w_hi, preferred_element_type=F32)
              + jnp.dot(z_hi, w_lo, preferred_element_type=F32))
    idx = lax.broadcasted_iota(jnp.int32, logits.shape, 1).astype(F32)
    logits = jnp.where(idx < n_e, logits, -jnp.inf)
    m1 = jnp.max(logits, axis=-1, keepdims=True)
    i1 = jnp.min(jnp.where(logits == m1, idx, float(LANES)), axis=-1, keepdims=True)
    rest = jnp.where(idx == i1, -jnp.inf, logits)
    m2 = jnp.max(rest, axis=-1, keepdims=True)
    i2 = jnp.min(jnp.where(rest == m2, idx, float(LANES)), axis=-1, keepdims=True)
    e2 = jnp.exp(m2 - m1)
    w1 = 1.0 / (1.0 + e2)
    w2 = e2 * w1

    @pl.when(pl.program_id(0) == 0)
    def _():
        run_ref[...] = jnp.zeros_like(run_ref)

    sel1, sel2 = idx == i1, idx == i2
    picked = jnp.where(sel1 | sel2, 1.0, 0.0)
    rows = picked.shape[0]
    earlier = (lax.broadcasted_iota(jnp.int32, (rows, rows), 0)
               > lax.broadcasted_iota(jnp.int32, (rows, rows), 1))
    before = jnp.dot(jnp.where(earlier, 1.0, 0.0).astype(BF16), picked.astype(BF16),
                     preferred_element_type=F32) + run_ref[...]
    rank1 = jnp.sum(jnp.where(sel1, before, 0.0), axis=-1, keepdims=True)
    rank2 = jnp.sum(jnp.where(sel2, before, 0.0), axis=-1, keepdims=True)
    run_ref[...] = run_ref[...] + jnp.sum(picked, axis=0, keepdims=True)
    counts_ref[...] = run_ref[...]

    route = jnp.zeros_like(logits)
    for col, val in enumerate((i1, i2, rank1, rank2, w1, w2)):
        route = jnp.where(idx == col, val, route)
    route_ref[...] = route[:, :ROUTE_COLS]


def _norm_mod(dm, x, g, mod, ci, rows, w_router=None):
    d = x.shape[1]
    tr = dm.tr
    in_specs = [pl.BlockSpec((tr, d), lambda i: (i, 0)),
                pl.BlockSpec((1, d), lambda i: (0, 0)),
                pl.BlockSpec((None, N_MOD, d), lambda i: (dm.group(i, tr), 0, 0))]
    h_spec = pl.BlockSpec((tr, d), lambda i: (i, 0))
    h_shape = jax.ShapeDtypeStruct((rows, d), BF16)
    if w_router is None:
        return pl.pallas_call(
            functools.partial(_norm_mod_kernel, ci=ci),
            out_shape=h_shape, grid=(rows // tr,), in_specs=in_specs, out_specs=h_spec,
            compiler_params=_cparams(1), name="norm_modulate",
        )(x, g.reshape(1, d), mod)
    n_e = w_router.shape[1]
    assert n_e <= LANES
    w_router = jnp.pad(w_router, ((0, 0), (0, LANES - n_e)))
    return pl.pallas_call(
        functools.partial(_norm_mod_router_kernel, ci=ci, n_e=n_e),
        out_shape=(jax.ShapeDtypeStruct((rows, d), F32),
                   jax.ShapeDtypeStruct((rows, ROUTE_COLS), F32),
                   jax.ShapeDtypeStruct((1, LANES), F32)),
        grid=(rows // tr,),
        in_specs=in_specs + [pl.BlockSpec((d, LANES), lambda i: (0, 0))],
        out_specs=(h_spec, pl.BlockSpec((tr, ROUTE_COLS), lambda i: (i, 0)),
                   pl.BlockSpec((1, LANES), lambda i: (0, 0))),
        scratch_shapes=[pltpu.VMEM((1, LANES), F32)],
        compiler_params=_cparams(1), name="norm_modulate_router",
    )(x, g.reshape(1, d), mod, w_router)


def _final_norm_kernel(x_ref, g_ref, o_ref):
    x = x_ref[...]
    o_ref[...] = x * lax.rsqrt(jnp.mean(x * x, axis=-1, keepdims=True) + NORM_EPS) * g_ref[...]


def _final_norm(dm, x, g):
    rows, d = x.shape
    tr = dm.tr
    return pl.pallas_call(
        _final_norm_kernel,
        out_shape=jax.ShapeDtypeStruct((rows, d), F32),
        grid=(rows // tr,),
        in_specs=[pl.BlockSpec((tr, d), lambda i: (i, 0)), pl.BlockSpec((1, d), lambda i: (0, 0))],
        out_specs=pl.BlockSpec((tr, d), lambda i: (i, 0)),
        compiler_params=_cparams(1), name="final_norm",
    )(x, g.reshape(1, d))


def _rope_epilogue(acc, cos_ref, sina_ref, sinb_ref, g_ref, o_ref, scale):
    cos, sina, sinb = cos_ref[...], sina_ref[...], sinb_ref[...]
    for c in range(acc.shape[1] // HEAD_DIM):
        xh = acc[:, c * HEAD_DIM:(c + 1) * HEAD_DIM]
        if g_ref is not None:
            xh = xh * lax.rsqrt(jnp.mean(xh * xh, axis=-1, keepdims=True) + NORM_EPS) * g_ref[...]
        fwd = pltpu.roll(xh, HEAD_DIM - ROPE_FREQS, 1)
        bwd = pltpu.roll(xh, ROPE_FREQS, 1)
        r = xh * cos + fwd * sina + bwd * sinb
        if scale != 1.0:
            r = r * scale
        o_ref[:, c * HEAD_DIM:(c + 1) * HEAD_DIM] = r.astype(o_ref.dtype)


def _mm_kernel(*refs, mode, n_a, lat_tiles, tail_rows, n_w, cast, n_extra, gate_idx, scale):
    a_refs = refs[:n_a]
    w_refs = refs[n_a:n_a + n_w]
    extra = refs[n_a + n_w:n_a + n_w + n_extra]
    o_ref = refs[n_a + n_w + n_extra]
    wb_refs = refs[n_a + n_w + n_extra + 1:]

    if cast:
        @pl.when(pl.program_id(1) == 0)
        def _():
            for w_ref, wb_ref in zip(w_refs, wb_refs):
                wb_ref[...] = w_ref[...].astype(BF16)
        w_srcs = wb_refs
    else:
        w_srcs = w_refs

    tm = o_ref.shape[0]
    if n_a == 1 and tail_rows == tm:
        _mm_tile(a_refs[0], tm, w_srcs, extra, o_ref, mode, gate_idx, scale)
        return

    i = pl.program_id(1)

    @pl.when(i < lat_tiles)
    def _():
        _mm_tile(a_refs[0], tm, w_srcs, extra, o_ref, mode, gate_idx, scale)

    @pl.when(i >= lat_tiles)
    def _():
        _mm_tile(a_refs[-1], tail_rows, w_srcs, extra, o_ref, mode, gate_idx, scale)


def _mm_tile(a_ref, nrows, w_srcs, extra, o_ref, mode, gate_idx, scale):
    rows = lambda ref: ref.at[0:nrows]
    a = a_ref[0:nrows, :]
    accs = [jnp.dot(a, w[...], preferred_element_type=F32) for w in w_srcs]
    n_extra = len(extra)
    o_ref = rows(o_ref)

    if mode == "plain":
        o_ref[...] = accs[0].astype(o_ref.dtype)
    elif mode == "rope":
        g_ref = extra[3] if n_extra == 4 else None
        _rope_epilogue(accs[0], rows(extra[0]), rows(extra[1]), rows(extra[2]), g_ref, o_ref, scale)
    elif mode == "resid":
        x_ref, mod_ref = extra
        gate = mod_ref[gate_idx:gate_idx + 1, :]
        o_ref[...] = rows(x_ref)[...] + gate * accs[0]
    elif mode == "swiglu":
        o_ref[...] = (_silu(accs[0]) * accs[1]).astype(o_ref.dtype)
    else:
        raise ValueError(mode)


def _pick_tn(k, n, w_itemsize, n_w, tm, io_bytes):
    best = LANES if n % LANES == 0 else n
    for tn in range(LANES, n + 1, LANES):
        w_blocks = n_w * k * tn * w_itemsize
        need = 2 * tm * k * 2 + 2 * w_blocks + (n_w * k * tn * 2 if w_itemsize == 4 else 0) + 2 * tm * tn * io_bytes
        if n % tn == 0 and w_blocks <= W_TILE_BYTES and need <= VMEM_TILE_BUDGET:
            best = tn
    return best


def _matmul(dm, a, w, *, rows, mode, out_dtype, e0=0, n_e=1, col0=0, n_cols=None, tn=None,
            extra=(), extra_specs=(), gate_idx=0, scale=1.0, swiglu_half=None, a_groups=False):
    a_parts = list(a) if isinstance(a, (list, tuple)) else [a]
    _, k, n_w_cols = w.shape
    assert all(p.shape[1] == (n_e * k if a_groups else k) for p in a_parts)
    cast = w.dtype != BF16
    n_w = 2 if mode == "swiglu" else 1
    if n_cols is None:
        n_cols = swiglu_half if mode == "swiglu" else n_w_cols - col0
    tm = dm.tm
    if tn is None:
        io_bytes = jnp.dtype(out_dtype).itemsize + (4 if mode == "resid" else 0)
        tn = _pick_tn(k, n_cols, w.dtype.itemsize, n_w, tm, io_bytes)
    assert n_cols % tn == 0 and col0 % tn == 0
    bpe = n_cols // tn
    blk0 = col0 // tn

    def w_map(off):
        if a_groups:
            return lambda j, i: (e0, 0, blk0 + off + j % bpe)
        return lambda j, i: (e0 + j // bpe, 0, blk0 + off + j % bpe)

    if len(a_parts) == 1:
        lat_tiles, tail_rows = divmod(rows, tm)
        row_tiles = lat_tiles + (1 if tail_rows else 0)
        tail_rows = tail_rows or tm
        a_specs = [pl.BlockSpec((tm, k), (lambda j, i: (i, j // bpe)) if a_groups else (lambda j, i: (i, 0)))]
    else:
        lat_rows, ctx_rows = a_parts[0].shape[0], a_parts[1].shape[0]
        tail_rows = min(tm, ctx_rows)
        assert not a_groups and lat_rows % tm == 0 and ctx_rows % tail_rows == 0
        assert rows == lat_rows + ctx_rows
        lat_tiles = lat_rows // tm
        row_tiles = lat_tiles + ctx_rows // tail_rows
        a_specs = [pl.BlockSpec((tm, k), lambda j, i: (jnp.minimum(i, lat_tiles - 1), 0)),
                   pl.BlockSpec((tail_rows, k), lambda j, i: (jnp.maximum(i - lat_tiles, 0), 0))]
    in_specs = a_specs + [pl.BlockSpec((None, k, tn), w_map(0))]
    args = a_parts + [w]
    if n_w == 2:
        assert swiglu_half % tn == 0
        in_specs.append(pl.BlockSpec((None, k, tn), w_map(swiglu_half // tn)))
        args.append(w)
    in_specs += list(extra_specs)
    args += list(extra)
    scratch = [pltpu.VMEM((k, tn), BF16) for _ in range(n_w)] if cast else []
    kern = functools.partial(_mm_kernel, mode=mode, n_a=len(a_parts), lat_tiles=lat_tiles,
                             tail_rows=tail_rows, n_w=n_w, cast=cast, n_extra=len(extra),
                             gate_idx=gate_idx, scale=scale)
    return pl.pallas_call(
        kern,
        out_shape=jax.ShapeDtypeStruct((rows, n_e * n_cols), out_dtype),
        grid=(n_e * bpe, row_tiles),
        in_specs=in_specs,
        out_specs=pl.BlockSpec((tm, tn), lambda j, i: (i, j)),
        scratch_shapes=scratch,
        compiler_params=_cparams(2),
        name="matmul_" + mode,
    )(*args)


def _proj_rope(dm, h, w, layer, col0, n_cols, rows, tabs, scale=1.0, gain=None):
    tm = dm.tm
    tab_spec = pl.BlockSpec((tm, HEAD_DIM), lambda j, i: (i, 0))
    extra, specs = list(tabs), [tab_spec] * 3
    if gain is not None:
        extra.append(gain.reshape(1, HEAD_DIM))
        specs.append(pl.BlockSpec((1, HEAD_DIM), lambda j, i: (0, 0)))
    return _matmul(dm, h, w, rows=rows, mode="rope", out_dtype=BF16, e0=layer, col0=col0,
                   n_cols=n_cols, extra=extra, extra_specs=specs, scale=scale)


def _proj_plain(dm, h, w, layer, col0, n_cols, rows):
    return _matmul(dm, h, w, rows=rows, mode="plain", out_dtype=BF16, e0=layer, col0=col0,
                   n_cols=n_cols)


def _proj_resid(dm, a, w, layer, x, mod, gate_idx, rows):
    tm = dm.tm
    _, k, n = w.shape
    tn = _pick_tn(k, n, w.dtype.itemsize, 1, tm, 8)
    specs = [pl.BlockSpec((tm, tn), lambda j, i: (i, j)),
             pl.BlockSpec((None, N_MOD, tn), lambda j, i: (dm.group(i, tm), 0, j))]
    return _matmul(dm, a, w, rows=rows, mode="resid", out_dtype=F32, e0=layer, tn=tn,
                   extra=[x, mod], extra_specs=specs, gate_idx=gate_idx)


def _route_tables(route, counts, n_e, rows):
    tg = MOE_TILE
    expert = route[:, 0:2].astype(jnp.int32)
    rank = route[:, 2:4].astype(jnp.int32)
    cnt = counts[0, :n_e].astype(jnp.int32)
    padded = (cnt + tg - 1) // tg * tg
    ends = jnp.cumsum(padded)
    slots = (ends - padded)[expert] + rank
    assert (2 * rows) % tg == 0
    n_tiles = 2 * rows // tg + n_e
    tok = jnp.broadcast_to(jnp.arange(rows, dtype=jnp.int32)[:, None], (rows, 2))
    src = jnp.zeros((n_tiles * tg,), jnp.int32).at[slots.reshape(-1)].set(tok.reshape(-1))
    tile_row0 = jnp.arange(n_tiles, dtype=jnp.int32)[:, None] * tg
    tile_expert = jnp.minimum(jnp.sum(tile_row0 >= ends[None, :], axis=1), n_e - 1).astype(jnp.int32)
    return slots, src, tile_expert, n_tiles


def _rows_copy(src_hbm, dst, sem, n):
    return pltpu.make_async_copy(src_hbm.at[pl.ds(0, n), :], dst, sem)


def _gather_rows(idx_ref, stride, offset, src_hbm, dst_ref, sem):
    def issue(r, carry):
        row = idx_ref[0, stride * r + offset]
        pltpu.make_async_copy(src_hbm.at[pl.ds(row, 1), :], dst_ref.at[pl.ds(r, 1), :], sem).start()
        return carry

    lax.fori_loop(0, dst_ref.shape[0], issue, 0, unroll=GATHER_UNROLL)


def _dispatch_kernel(cur_ref, nxt_ref, z_hbm, o_ref, buf_ref, sem):
    g = pl.program_id(0)
    slot = g % 2

    @pl.when(g == 0)
    def _():
        _gather_rows(cur_ref, 1, 0, z_hbm, buf_ref.at[0], sem.at[0])

    @pl.when(g + 1 < pl.num_programs(0))
    def _():
        _gather_rows(nxt_ref, 1, 0, z_hbm, buf_ref.at[1 - slot], sem.at[1 - slot])

    _rows_copy(z_hbm, buf_ref.at[slot], sem.at[slot], buf_ref.shape[1]).wait()
    o_ref[...] = buf_ref[slot].astype(o_ref.dtype)


def _dispatch(z, src, n_tiles):
    tg = MOE_TILE
    d = z.shape[1]
    src = src.reshape(n_tiles, 1, tg)
    return pl.pallas_call(
        _dispatch_kernel,
        out_shape=jax.ShapeDtypeStruct((n_tiles * tg, d), BF16),
        grid=(n_tiles,),
        in_specs=[pl.BlockSpec((None, 1, tg), lambda g: (g, 0, 0), memory_space=pltpu.SMEM),
                  pl.BlockSpec((None, 1, tg), lambda g: (jnp.minimum(g + 1, n_tiles - 1), 0, 0),
                               memory_space=pltpu.SMEM),
                  pl.BlockSpec(memory_space=pl.ANY)],
        out_specs=pl.BlockSpec((tg, d), lambda g: (g, 0)),
        scratch_shapes=[pltpu.VMEM((2, tg, d), z.dtype), pltpu.SemaphoreType.DMA((2,))],
        compiler_params=_cparams(1), name="moe_dispatch",
    )(src, src, z)


def _expert_changed(te_ref):
    g = pl.program_id(1)
    return (g == 0) | (te_ref[g] != te_ref[jnp.maximum(g - 1, 0)])


def _moe_up_kernel(te_ref, a_ref, wg_ref, wu_ref, o_ref, wgb_ref, wub_ref):
    @pl.when(_expert_changed(te_ref))
    def _():
        wgb_ref[...] = wg_ref[...].astype(BF16)
        wub_ref[...] = wu_ref[...].astype(BF16)

    a = a_ref[...]
    hg = jnp.dot(a, wgb_ref[...], preferred_element_type=F32)
    hu = jnp.dot(a, wub_ref[...], preferred_element_type=F32)
    o_ref[...] = (_silu(hg) * hu).astype(o_ref.dtype)


def _moe_down_kernel(te_ref, a_ref, w_ref, o_ref, wb_ref):
    @pl.when(_expert_changed(te_ref))
    def _():
        wb_ref[...] = w_ref[...].astype(BF16)

    o_ref[...] = jnp.dot(a_ref[...], wb_ref[...], preferred_element_type=F32)


def _moe_up(zg, w_in, e0, tile_expert, f):
    tg = MOE_TILE
    s, d = zg.shape
    tn = min(f, MOE_UP_COLS)
    assert f % tn == 0
    nb = f // tn
    grid_spec = pltpu.PrefetchScalarGridSpec(
        num_scalar_prefetch=1, grid=(nb, s // tg),
        in_specs=[pl.BlockSpec((tg, d), lambda j, g, te: (g, 0)),
                  pl.BlockSpec((None, d, tn), lambda j, g, te: (e0 + te[g], 0, j)),
                  pl.BlockSpec((None, d, tn), lambda j, g, te: (e0 + te[g], 0, nb + j))],
        out_specs=pl.BlockSpec((tg, tn), lambda j, g, te: (g, j)),
        scratch_shapes=[pltpu.VMEM((d, tn), BF16), pltpu.VMEM((d, tn), BF16)])
    return pl.pallas_call(
        _moe_up_kernel, out_shape=jax.ShapeDtypeStruct((s, f), BF16), grid_spec=grid_spec,
        compiler_params=_cparams(2), name="moe_up",
    )(tile_expert, zg, w_in, w_in)


def _moe_down(hdn, w_out, e0, tile_expert):
    tg = MOE_TILE
    s, f = hdn.shape
    d = w_out.shape[2]
    tn = _pick_tn(f, d, 4, 1, tg, 4)
    grid_spec = pltpu.PrefetchScalarGridSpec(
        num_scalar_prefetch=1, grid=(d // tn, s // tg),
        in_specs=[pl.BlockSpec((tg, f), lambda j, g, te: (g, 0)),
                  pl.BlockSpec((None, f, tn), lambda j, g, te: (e0 + te[g], 0, j))],
        out_specs=pl.BlockSpec((tg, tn), lambda j, g, te: (g, j)),
        scratch_shapes=[pltpu.VMEM((f, tn), BF16)])
    return pl.pallas_call(
        _moe_down_kernel, out_shape=jax.ShapeDtypeStruct((s, d), F32), grid_spec=grid_spec,
        compiler_params=_cparams(2), name="moe_down",
    )(tile_expert, hdn, w_out)


def _combine_kernel(cur_ref, nxt_ref, y_hbm, x_ref, route_ref, mod_ref, o_ref, y_ref, sem, *, gate_idx):
    i = pl.program_id(0)
    slot = i % 2
    tr = x_ref.shape[0]

    def start(idx_ref, s):
        for c in range(2):
            _gather_rows(idx_ref, 2, c, y_hbm, y_ref.at[s, c], sem.at[s, c])

    @pl.when(i == 0)
    def _():
        start(cur_ref, 0)

    @pl.when(i + 1 < pl.num_programs(0))
    def _():
        start(nxt_ref, 1 - slot)

    for c in range(2):
        _rows_copy(y_hbm, y_ref.at[slot, c], sem.at[slot, c], tr).wait()
    route = route_ref[...]
    gate = mod_ref[gate_idx:gate_idx + 1, :]
    o_ref[...] = x_ref[...] + gate * (route[:, 4:5] * y_ref[slot, 0] + route[:, 5:6] * y_ref[slot, 1])


def _moe_combine(dm, yg, slots, x, route, mod, gate_idx, rows):
    tr = COMBINE_ROW_TILE
    d = x.shape[1]
    n = rows // tr
    slots = slots.reshape(n, 1, 2 * tr)
    return pl.pallas_call(
        functools.partial(_combine_kernel, gate_idx=gate_idx),
        out_shape=jax.ShapeDtypeStruct((rows, d), F32),
        grid=(n,),
        in_specs=[pl.BlockSpec((None, 1, 2 * tr), lambda i: (i, 0, 0), memory_space=pltpu.SMEM),
                  pl.BlockSpec((None, 1, 2 * tr), lambda i: (jnp.minimum(i + 1, n - 1), 0, 0),
                               memory_space=pltpu.SMEM),
                  pl.BlockSpec(memory_space=pl.ANY),
                  pl.BlockSpec((tr, d), lambda i: (i, 0)),
                  pl.BlockSpec((tr, ROUTE_COLS), lambda i: (i, 0)),
                  pl.BlockSpec((None, N_MOD, d), lambda i: (dm.group(i, tr), 0, 0))],
        out_specs=pl.BlockSpec((tr, d), lambda i: (i, 0)),
        scratch_shapes=[pltpu.VMEM((2, 2, tr, d), F32), pltpu.SemaphoreType.DMA((2, 2))],
        compiler_params=_cparams(1), name="moe_combine",
    )(slots, slots, yg, x, route, mod)


_NT = (((1,), (1,)), ((), ()))


def _key_chunks(refs):
    out, off = [], 0
    for r in refs:
        n = r.shape[0]
        ck = min(KEY_CHUNK, n)
        for r0 in range(0, n, ck):
            out.append((r, r0, ck, off))
            off += ck
    return out


def _lane_fold(acc, v, op):
    for t in range(v.shape[1] // LANES):
        piece = v[:, t * LANES:(t + 1) * LANES]
        acc = piece if acc is None else op(acc, piece)
    return acc


def _scores_pass(q, k_refs, lo, hi, s_ref):
    mx = None
    for r, r0, ck, off in _key_chunks(k_refs):
        s = lax.dot_general(q, r[r0:r0 + ck, lo:hi], _NT, preferred_element_type=F32)
        s_ref[:, off:off + ck] = s
        mx = _lane_fold(mx, s, jnp.maximum)
    return jnp.max(mx, axis=-1, keepdims=True)


def _values_pass(s_ref, m, v_refs, lo, hi):
    acc, ls = None, None
    for r, r0, ck, off in _key_chunks(v_refs):
        e = jnp.exp2(s_ref[:, off:off + ck] - m)
        ls = _lane_fold(ls, e, jnp.add)
        pv = jnp.dot(e.astype(BF16), r[r0:r0 + ck, lo:hi], preferred_element_type=F32)
        acc = pv if acc is None else acc + pv
    return acc * (1.0 / jnp.sum(ls, axis=-1, keepdims=True))


def _diff_attn_body(q_ref, k_refs, v_refs, lam_ref, g_ref, o_ref, s_refs, sub, lam_init):
    lv = lam_ref[...]
    lam = (jnp.exp(jnp.sum(lv[0:1] * lv[1:2], axis=-1, keepdims=True))
           - jnp.exp(jnp.sum(lv[2:3] * lv[3:4], axis=-1, keepdims=True)) + lam_init)
    n_sub = q_ref.shape[0] // sub
    chains = [(t, c) for t in range(n_sub) for c in range(2)]
    m = {}
    for t, c in chains:
        lo, hi = c * HEAD_DIM, (c + 1) * HEAD_DIM
        m[t, c] = _scores_pass(q_ref[t * sub:(t + 1) * sub, lo:hi], k_refs, lo, hi, s_refs[2 * t + c])
    outs = {ch: _values_pass(s_refs[2 * ch[0] + ch[1]], m[ch], v_refs, 0, 2 * HEAD_DIM) for ch in chains}
    for t in range(n_sub):
        o = outs[t, 0] - lam * outs[t, 1]
        o = o * lax.rsqrt(jnp.mean(o * o, axis=-1, keepdims=True) + NORM_EPS) * g_ref[...]
        o_ref[t * sub:(t + 1) * sub, :] = (o * (1.0 - lam_init)).astype(o_ref.dtype)


def _gqa_attn_body(q_ref, k_refs, v_refs, o_ref, s_refs):
    m = [_scores_pass(q_ref[:, g * HEAD_DIM:(g + 1) * HEAD_DIM], k_refs, 0, HEAD_DIM, s_refs[g])
         for g in range(GQA_GROUP)]
    for g in range(GQA_GROUP):
        o = _values_pass(s_refs[g], m[g], v_refs, 0, HEAD_DIM)
        o_ref[:, g * HEAD_DIM:(g + 1) * HEAD_DIM] = o.astype(o_ref.dtype)


def _attn_kernel(q_ref, *rest, kind, n_src, n_chains, sub, lam_init):
    k_refs = rest[:n_src]
    v_refs = rest[n_src:2 * n_src]
    params = rest[2 * n_src:-n_chains - 1]
    o_ref = rest[-n_chains - 1]
    s_refs = rest[-n_chains:]
    if kind == "diff":
        _diff_attn_body(q_ref, k_refs, v_refs, params[0], params[1], o_ref, s_refs, sub, lam_init)
    else:
        _gqa_attn_body(q_ref, k_refs, v_refs, o_ref, s_refs)


def _attention_call(dm, q, k, v, *, kind, latent, params, lam_init):
    sub = dm.tq
    n_sub = DIFF_SUBTILES if (kind == "diff" and latent and dm.seq % (DIFF_SUBTILES * sub) == 0) else 1
    tq = sub * n_sub
    wq = 2 * HEAD_DIM if kind == "diff" else GQA_GROUP * HEAD_DIM
    wk = 2 * HEAD_DIM if kind == "diff" else HEAD_DIM
    n_heads = k.shape[1] // wk
    lat_per_ctx = dm.m_lat // dm.ctx
    lat_spec = pl.BlockSpec((dm.seq, wk), lambda b, h, qi: (b, h))
    ctx_spec = pl.BlockSpec((dm.ctx, wk), lambda b, h, qi: (lat_per_ctx + b, h))
    if latent:
        n_q, row0, srcs, n_keys = dm.seq // tq, 0, [lat_spec, ctx_spec], dm.seq + dm.ctx
    else:
        n_q, row0, srcs, n_keys = dm.ctx // tq, dm.m_lat // tq, [ctx_spec], dm.ctx
    q_spec = pl.BlockSpec((tq, wq), lambda b, h, qi: (row0 + b * n_q + qi, h))
    in_specs = [q_spec] + srcs + srcs + [pl.BlockSpec(p.shape, lambda b, h, qi: (0, 0)) for p in params]
    args = [q] + [k] * len(srcs) + [v] * len(srcs) + list(params)
    n_chains = (2 if kind == "diff" else GQA_GROUP) * n_sub
    return pl.pallas_call(
        functools.partial(_attn_kernel, kind=kind, n_src=len(srcs), n_chains=n_chains, sub=sub,
                          lam_init=lam_init),
        out_shape=jax.ShapeDtypeStruct((dm.b * n_q * tq, q.shape[1]), BF16),
        grid=(dm.b, n_heads, n_q),
        in_specs=in_specs,
        out_specs=pl.BlockSpec((tq, wq), lambda b, h, qi: (b * n_q + qi, h)),
        scratch_shapes=[pltpu.VMEM((sub, n_keys), F32) for _ in range(n_chains)],
        compiler_params=_cparams(3),
        name=kind + ("_attention" if latent else "_attention_ctx"),
    )(*args)


def _attention(dm, q, k, v, *, kind, ctx_out, params=(), lam_init=0.0):
    common = dict(kind=kind, params=params, lam_init=lam_init)
    y = [_attention_call(dm, q, k, v, latent=True, **common)]
    if ctx_out:
        y.append(_attention_call(dm, q, k, v, latent=False, **common))
    return y


def _dft_tables(n, norm):
    n0 = 1
    while n0 * n0 * 4 <= n and n % (n0 * 2) == 0:
        n0 *= 2
    n1 = n // n0
    k = jnp.arange(n, dtype=jnp.int32)[:, None]

    def cs(m):
        ang = ((k * m) % n).astype(F32) * (2.0 * math.pi / n)
        return jnp.cos(ang), jnp.sin(ang)

    c1, s1 = cs(jnp.arange(n1, dtype=jnp.int32)[None, :] * n0)
    c0, s0 = cs(jnp.arange(n0, dtype=jnp.int32)[None, :])
    c1, s1, c0, s0 = c1[:, :, None], s1[:, :, None], c0[:, None, :], s0[:, None, :]
    cos = (c1 * c0 - s1 * s0).reshape(n, n) * norm
    sin = (s1 * c0 + c1 * s0).reshape(n, n) * norm
    return cos, sin


def _dft_rows_kernel(c_ref, s_ref, yc_ref, ys_ref, o_ref):
    o_ref[...] = (jnp.dot(c_ref[...], yc_ref[...], preferred_element_type=F32)
                  + jnp.dot(s_ref[...], ys_ref[...], preferred_element_type=F32)).astype(o_ref.dtype)


def _dft_rows(dm, y, n, row0):
    d = dm.d
    dg = d // FOURIER_GROUPS
    cos, sin = _dft_tables(n, n ** -0.5)
    cos, nsin = cos.astype(BF16), (-sin).astype(BF16)
    tmf = min(DFT_ROW_TILE, n)
    tn = min(512, dg)
    lb = dg // tn
    rb0 = row0 // n

    def y_map(off):
        return lambda b, j, i: (rb0 + b, (j // lb) * 2 * lb + off + j % lb)

    in_specs = [pl.BlockSpec((tmf, n), lambda b, j, i: (i, 0)),
                pl.BlockSpec((tmf, n), lambda b, j, i: (i, 0)),
                pl.BlockSpec((n, tn), y_map(0)),
                pl.BlockSpec((n, tn), y_map(lb))]
    tiles = n // tmf
    return pl.pallas_call(
        _dft_rows_kernel,
        out_shape=jax.ShapeDtypeStruct((dm.b * n, d), BF16),
        grid=(dm.b, d // tn, tiles),
        in_specs=in_specs,
        out_specs=pl.BlockSpec((tmf, tn), lambda b, j, i: (b * tiles + i, j)),
        compiler_params=_cparams(3),
        name="dft_positions",
    )(cos, nsin, y, y)


def _fourier_mix(dm, h, ctx_out):
    d = dm.d
    dg = d // FOURIER_GROUPS
    rows = h.shape[0]
    cos_c, sin_c = _dft_tables(dg, dg ** -0.5)
    cs = jnp.concatenate([cos_c, sin_c], axis=1).astype(BF16)[None]
    y = _matmul(dm, h, cs, rows=rows, mode="plain", out_dtype=BF16, n_e=FOURIER_GROUPS,
                a_groups=True, tn=min(2 * dg, 1024))
    f = [_dft_rows(dm, y, dm.seq, 0)]
    if ctx_out:
        f.append(_dft_rows(dm, y, dm.ctx, dm.m_lat))
    return f


def _rope_tables(dm):
    rows = dm.seq // GRID_W
    r, col = jnp.meshgrid(jnp.arange(rows), jnp.arange(GRID_W), indexing="ij")
    pos = jnp.stack([r.reshape(-1), col.reshape(-1)], axis=-1).astype(F32)
    inv_freq = 1.0 / (ROPE_THETA ** (jnp.arange(ROPE_FREQS, dtype=F32) / ROPE_FREQS))
    ang = pos[:, :, None] * inv_freq
    cos, sin = jnp.cos(ang), jnp.sin(ang)
    zero = jnp.zeros_like(sin)
    cos_t = jnp.stack([cos, cos], axis=2).reshape(dm.seq, HEAD_DIM)
    sina_t = jnp.stack([-sin, zero], axis=2).reshape(dm.seq, HEAD_DIM)
    sinb_t = jnp.stack([zero, sin], axis=2).reshape(dm.seq, HEAD_DIM)

    def full(t, fill):
        return jnp.concatenate([jnp.tile(t, (dm.b, 1)), jnp.full((dm.m_ctx, HEAD_DIM), fill, F32)], axis=0)

    return full(cos_t, 1.0), full(sina_t, 0.0), full(sinb_t, 0.0)


def kernel(x, c, ctx, c_ctx, w_mod, b_mod, norm_g, diff_w_in, diff_w_out, diff_lambda, diff_subln_g,
           fourier_w_out, gqa_w_in, gqa_w_out, gqa_qk_g, ffn_w_in, ffn_w_out, moe_router, moe_w_in,
           moe_w_out, final_g):
    dm = _Dims(x, ctx)
    b, d = dm.b, dm.d
    depth = w_mod.shape[0]
    q_scale = HEAD_DIM ** -0.5 * LOG2E

    cc = jnp.concatenate([c, c_ctx[None, :], jnp.zeros((8 - b - 1, d), F32)], axis=0)
    mods = _modulations(cc, w_mod, b_mod)[:, :b + 1].reshape(depth, b + 1, N_MOD, d)
    tabs = _rope_tables(dm)
    xs = jnp.concatenate([x.reshape(dm.m_lat, d), ctx.reshape(dm.m_ctx, d)], axis=0)

    n_moe, n_e, _, two_f = moe_w_in.shape
    moe_w_in = moe_w_in.reshape(n_moe * n_e, d, two_f)
    moe_w_out = moe_w_out.reshape(n_moe * n_e, two_f // 2, d)

    for i in range(depth):
        last = i == depth - 1
        mod = mods[i]
        rows_in = xs.shape[0]
        rows_out = dm.m_lat if last else dm.m_all
        h = _norm_mod(dm, xs, norm_g[i, 0], mod, 0, rows_in)
        kind, j = i % N_MIXERS, i // N_MIXERS
        if kind == 0:
            qk = diff_w_in.shape[2] // 3
            q = _proj_rope(dm, h, diff_w_in, j, 0, qk, rows_out, tabs, scale=q_scale)
            k = _proj_rope(dm, h, diff_w_in, j, qk, qk, rows_in, tabs)
            v = _proj_plain(dm, h, diff_w_in, j, 2 * qk, qk, rows_in)
            lam_init = 0.8 - 0.6 * math.exp(-0.3 * i)
            y = _attention(dm, q, k, v, kind="diff", ctx_out=not last,
                           params=(diff_lambda[j], diff_subln_g[j].reshape(1, 2 * HEAD_DIM)),
                           lam_init=lam_init)
            w_out = diff_w_out
        elif kind == 1:
            y = _fourier_mix(dm, h, not last)
            w_out = fourier_w_out
        else:
            kvd = (gqa_w_in.shape[2] - d) // 2
            q = _proj_rope(dm, h, gqa_w_in, j, 0, d, rows_out, tabs, scale=q_scale, gain=gqa_qk_g[j, 0])
            k = _proj_rope(dm, h, gqa_w_in, j, d, kvd, rows_in, tabs, gain=gqa_qk_g[j, 1])
            v = _proj_plain(dm, h, gqa_w_in, j, d + kvd, kvd, rows_in)
            y = _attention(dm, q, k, v, kind="gqa", ctx_out=not last)
            w_out = gqa_w_out
        xs = _proj_resid(dm, y, w_out, j, xs, mod, 2, rows_out)

        f = i // 2
        if i % 2 == 0:
            z = _norm_mod(dm, xs, norm_g[i, 1], mod, 3, rows_out)
            hdn = _matmul(dm, z, ffn_w_in, rows=rows_out, mode="swiglu", out_dtype=BF16, e0=f,
                          swiglu_half=ffn_w_in.shape[2] // 2)
            xs = _proj_resid(dm, hdn, ffn_w_out, f, xs, mod, 5, rows_out)
        else:
            z, route, counts = _norm_mod(dm, xs, norm_g[i, 1], mod, 3, rows_out, w_router=moe_router[f])
            slots, src, tile_expert, n_tiles = _route_tables(route, counts, n_e, rows_out)
            zg = _dispatch(z, src, n_tiles)
            hdn = _moe_up(zg, moe_w_in, f * n_e, tile_expert, two_f // 2)
            yg = _moe_down(hdn, moe_w_out, f * n_e, tile_expert)
            xs = _moe_combine(dm, yg, slots, xs, route, mod, 5, rows_out)

    return _final_norm(dm, xs, final_g).reshape(b, dm.seq, d)
```

```python
import functools
import math

import jax
import jax.numpy as jnp
from jax import lax
from jax.experimental import pallas as pl
from jax.experimental.pallas import tpu as pltpu

HEAD_DIM = 128
GRID_W = 64
ROPE_THETA = 10000.0
ROPE_FREQS = HEAD_DIM // 4
NORM_EPS = 1e-6
N_MOD = 6
N_MIXERS = 3
FOURIER_GROUPS = 4
GQA_GROUP = 4

LANES = 128
VMEM_LIMIT_BYTES = 56 * 2**20
ROW_TILES = (256, 512)
DFT_ROW_TILE = 512
VMEM_TILE_BUDGET = 48 * 2**20
NORM_ROW_TILE = 256
NORM_WIDE_ROW_TILE = 512
MOE_TILE = 256
MOE_UP_COLS = 512
COMBINE_ROW_TILE = 256
GATHER_UNROLL = 8
DMA_PRIORITIES = 2
Q_TILE = 256
DIFF_SUBTILES = 2
KEY_CHUNK = 512
LOG2E = math.log2(math.e)
W_TILE_BYTES = 16 * 2**20

F32 = jnp.float32
BF16 = jnp.bfloat16


def _cparams(n_axes):
    return pltpu.CompilerParams(dimension_semantics=("arbitrary",) * n_axes,
                                vmem_limit_bytes=VMEM_LIMIT_BYTES)


def _silu(v):
    return v / (1.0 + jnp.exp(-v))


class _Dims:
    def __init__(self, x, ctx):
        self.b, self.seq, self.d = x.shape
        self.ctx = ctx.shape[1]
        self.m_lat = self.b * self.seq
        self.m_ctx = self.b * self.ctx
        self.m_all = self.m_lat + self.m_ctx
        g = math.gcd(self.seq, self.m_ctx)
        self.tm_options = sorted({math.gcd(t, self.seq) for t in ROW_TILES})
        self.tr = min(NORM_ROW_TILE, g)
        self.tr_wide = min(NORM_WIDE_ROW_TILE, g)
        self.tq = min(Q_TILE, self.ctx)
        assert self.seq % self.tq == 0 and self.ctx % self.tq == 0

    def group(self, i, tile):
        r = i * tile
        return jnp.where(r < self.m_lat, r // self.seq, self.b)


def _mod_kernel(c_ref, w_ref, b_ref, o_ref):
    s = _silu(c_ref[...]).astype(BF16)
    w = w_ref[...].astype(BF16)
    o_ref[...] = jnp.dot(s, w, preferred_element_type=F32) + b_ref[...]


def _modulations(cc, w_mod, b_mod):
    depth, d, n = w_mod.shape
    tn = 512 if n % 512 == 0 else n
    rows = cc.shape[0]
    return pl.pallas_call(
        _mod_kernel,
        out_shape=jax.ShapeDtypeStruct((depth, rows, n), F32),
        grid=(depth, n // tn),
        in_specs=[pl.BlockSpec((rows, d), lambda l, j: (0, 0)),
                  pl.BlockSpec((None, d, tn), lambda l, j: (l, 0, j)),
                  pl.BlockSpec((None, 1, tn), lambda l, j: (l, 0, j))],
        out_specs=pl.BlockSpec((None, rows, tn), lambda l, j: (l, 0, j)),
        compiler_params=_cparams(2),
        name="adaln_modulations",
    )(cc, w_mod, b_mod.reshape(depth, 1, n))


def _norm_mod_value(x_ref, g_ref, mod_ref, ci):
    x = x_ref[...]
    y = x * lax.rsqrt(jnp.mean(x * x, axis=-1, keepdims=True) + NORM_EPS) * g_ref[...]
    shift = mod_ref[ci:ci + 1, :]
    scale = mod_ref[ci + 1:ci + 2, :]
    return y * (1.0 + scale) + shift


def _norm_mod_kernel(x_ref, g_ref, mod_ref, o_ref, *, ci):
    o_ref[...] = _norm_mod_value(x_ref, g_ref, mod_ref, ci).astype(BF16)


def _split_bf16(v):
    hi = v.astype(BF16)
    lo = (v - hi.astype(F32)).astype(BF16)
    return hi, lo


ROUTE_COLS = 8


def _norm_mod_router_kernel(x_ref, g_ref, mod_ref, wr_ref, o_ref, route_ref, counts_ref, run_ref, *, ci, n_e):
    z = _norm_mod_value(x_ref, g_ref, mod_ref, ci)
    o_ref[...] = z
    z_hi, z_lo = _split_bf16(z)
    w_hi, w_lo = _split_bf16(wr_ref[...])
    logits = (jnp.dot(z_hi, w_hi, preferred_element_type=F32)
              + jnp.dot(z_lo, w_hi, preferred_element_type=F32)
              + jnp.dot(z_hi, w_lo, preferred_element_type=F32))
    idx = lax.broadcasted_iota(jnp.int32, logits.shape, 1).astype(F32)
    logits = jnp.where(idx < n_e, logits, -jnp.inf)
    m1 = jnp.max(logits, axis=-1, keepdims=True)
    i1 = jnp.min(jnp.where(logits == m1, idx, float(LANES)), axis=-1, keepdims=True)
    rest = jnp.where(idx == i1, -jnp.inf, logits)
    m2 = jnp.max(rest, axis=-1, keepdims=True)
    i2 = jnp.min(jnp.where(rest == m2, idx, float(LANES)), axis=-1, keepdims=True)
    e2 = jnp.exp(m2 - m1)
    w1 = 1.0 / (1.0 + e2)
    w2 = e2 * w1

    @pl.when(pl.program_id(0) == 0)
    def _():
        run_ref[...] = jnp.zeros_like(run_ref)

    sel1, sel2 = idx == i1, idx == i2
    picked = jnp.where(sel1 | sel2, 1.0, 0.0)
    rows = picked.shape[0]
    earlier = (lax.broadcasted_iota(jnp.int32, (rows, rows), 0)
               > lax.broadcasted_iota(jnp.int32, (rows, rows), 1))
    before = jnp.dot(jnp.where(earlier, 1.0, 0.0).astype(BF16), picked.astype(BF16),
                     preferred_element_type=F32) + run_ref[...]
    rank1 = jnp.sum(jnp.where(sel1, before, 0.0), axis=-1, keepdims=True)
    rank2 = jnp.sum(jnp.where(sel2, before, 0.0), axis=-1, keepdims=True)
    run_ref[...] = run_ref[...] + jnp.sum(picked, axis=0, keepdims=True)
    counts_ref[...] = run_ref[...]

    route = jnp.zeros_like(logits)
    for col, val in enumerate((i1, i2, rank1, rank2, w1, w2)):
        route = jnp.where(idx == col, val, route)
    route_ref[...] = route[:, :ROUTE_COLS]


def _norm_mod(dm, x, g, mod, ci, rows, w_router=None):
    d = x.shape[1]
    tr = dm.tr if w_router is not None else dm.tr_wide
    in_specs = [pl.BlockSpec((tr, d), lambda i: (i, 0)),
                pl.BlockSpec((1, d), lambda i: (0, 0)),
                pl.BlockSpec((None, N_MOD, d), lambda i: (dm.group(i, tr), 0, 0))]
    h_spec = pl.BlockSpec((tr, d), lambda i: (i, 0))
    h_shape = jax.ShapeDtypeStruct((rows, d), BF16)
    if w_router is None:
        return pl.pallas_call(
            functools.partial(_norm_mod_kernel, ci=ci),
            out_shape=h_shape, grid=(rows // tr,), in_specs=in_specs, out_specs=h_spec,
            compiler_params=_cparams(1), name="norm_modulate",
        )(x, g.reshape(1, d), mod)
    n_e = w_router.shape[1]
    assert n_e <= LANES
    w_router = jnp.pad(w_router, ((0, 0), (0, LANES - n_e)))
    return pl.pallas_call(
        functools.partial(_norm_mod_router_kernel, ci=ci, n_e=n_e),
        out_shape=(jax.ShapeDtypeStruct((rows, d), F32),
                   jax.ShapeDtypeStruct((rows, ROUTE_COLS), F32),
                   jax.ShapeDtypeStruct((1, LANES), F32)),
        grid=(rows // tr,),
        in_specs=in_specs + [pl.BlockSpec((d, LANES), lambda i: (0, 0))],
        out_specs=(h_spec, pl.BlockSpec((tr, ROUTE_COLS), lambda i: (i, 0)),
                   pl.BlockSpec((1, LANES), lambda i: (0, 0))),
        scratch_shapes=[pltpu.VMEM((1, LANES), F32)],
        compiler_params=_cparams(1), name="norm_modulate_router",
    )(x, g.reshape(1, d), mod, w_router)


def _final_norm_kernel(x_ref, g_ref, o_ref):
    x = x_ref[...]
    o_ref[...] = x * lax.rsqrt(jnp.mean(x * x, axis=-1, keepdims=True) + NORM_EPS) * g_ref[...]


def _final_norm(dm, x, g):
    rows, d = x.shape
    tr = dm.tr
    return pl.pallas_call(
        _final_norm_kernel,
        out_shape=jax.ShapeDtypeStruct((rows, d), F32),
        grid=(rows // tr,),
        in_specs=[pl.BlockSpec((tr, d), lambda i: (i, 0)), pl.BlockSpec((1, d), lambda i: (0, 0))],
        out_specs=pl.BlockSpec((tr, d), lambda i: (i, 0)),
        compiler_params=_cparams(1), name="final_norm",
    )(x, g.reshape(1, d))


def _rope_epilogue(acc, cos_ref, sina_ref, sinb_ref, g_ref, o_ref, scale):
    cos, sina, sinb = cos_ref[...], sina_ref[...], sinb_ref[...]
    for c in range(acc.shape[1] // HEAD_DIM):
        xh = acc[:, c * HEAD_DIM:(c + 1) * HEAD_DIM]
        if g_ref is not None:
            xh = xh * lax.rsqrt(jnp.mean(xh * xh, axis=-1, keepdims=True) + NORM_EPS) * g_ref[...]
        fwd = pltpu.roll(xh, HEAD_DIM - ROPE_FREQS, 1)
        bwd = pltpu.roll(xh, ROPE_FREQS, 1)
        r = xh * cos + fwd * sina + bwd * sinb
        if scale != 1.0:
            r = r * scale
        o_ref[:, c * HEAD_DIM:(c + 1) * HEAD_DIM] = r.astype(o_ref.dtype)


def _mm_kernel(*refs, mode, n_a, lat_tiles, tail_rows, n_w, cast, n_extra, gate_idx, scale):
    a_refs = refs[:n_a]
    w_refs = refs[n_a:n_a + n_w]
    extra = refs[n_a + n_w:n_a + n_w + n_extra]
    o_ref = refs[n_a + n_w + n_extra]
    wb_refs = refs[n_a + n_w + n_extra + 1:]

    if cast:
        @pl.when(pl.program_id(1) == 0)
        def _():
            for w_ref, wb_ref in zip(w_refs, wb_refs):
                wb_ref[...] = w_ref[...].astype(BF16)
        w_srcs = wb_refs
    else:
        w_srcs = w_refs

    tm = o_ref.shape[0]
    if n_a == 1 and tail_rows == tm:
        _mm_tile(a_refs[0], tm, w_srcs, extra, o_ref, mode, gate_idx, scale)
        return

    i = pl.program_id(1)

    @pl.when(i < lat_tiles)
    def _():
        _mm_tile(a_refs[0], tm, w_srcs, extra, o_ref, mode, gate_idx, scale)

    @pl.when(i >= lat_tiles)
    def _():
        _mm_tile(a_refs[-1], tail_rows, w_srcs, extra, o_ref, mode, gate_idx, scale)


def _mm_tile(a_ref, nrows, w_srcs, extra, o_ref, mode, gate_idx, scale):
    rows = lambda ref: ref.at[0:nrows]
    a = a_ref[0:nrows, :]
    accs = [jnp.dot(a, w[...], preferred_element_type=F32) for w in w_srcs]
    n_extra = len(extra)
    o_ref = rows(o_ref)

    if mode == "plain":
        o_ref[...] = accs[0].astype(o_ref.dtype)
    elif mode == "rope":
        g_ref = extra[3] if n_extra == 4 else None
        _rope_epilogue(accs[0], rows(extra[0]), rows(extra[1]), rows(extra[2]), g_ref, o_ref, scale)
    elif mode == "resid":
        x_ref, mod_ref = extra
        gate = mod_ref[gate_idx:gate_idx + 1, :]
        o_ref[...] = rows(x_ref)[...] + gate * accs[0]
    elif mode == "swiglu":
        o_ref[...] = (_silu(accs[0]) * accs[1]).astype(o_ref.dtype)
    else:
        raise ValueError(mode)


def _pick_tn(k, n, w_itemsize, n_w, tm, io_bytes):
    best = LANES if n % LANES == 0 else n
    for tn in range(LANES, n + 1, LANES):
        w_blocks = n_w * k * tn * w_itemsize
        need = 2 * tm * k * 2 + 2 * w_blocks + (n_w * k * tn * 2 if w_itemsize == 4 else 0) + 2 * tm * tn * io_bytes
        if n % tn == 0 and w_blocks <= W_TILE_BYTES and need <= VMEM_TILE_BUDGET:
            best = tn
    return best


def _pick_tiles(dm, k, n, w_itemsize, n_w, io_bytes):
    options = [(_pick_tn(k, n, w_itemsize, n_w, tm, io_bytes), tm) for tm in dm.tm_options]
    tn, tm = max(options)
    return tm, tn


def _matmul(dm, a, w, *, rows, mode, out_dtype, e0=0, n_e=1, col0=0, n_cols=None, tm=None, tn=None,
            extra=(), extra_specs=(), gate_idx=0, scale=1.0, swiglu_half=None, a_groups=False):
    a_parts = list(a) if isinstance(a, (list, tuple)) else [a]
    _, k, n_w_cols = w.shape
    assert all(p.shape[1] == (n_e * k if a_groups else k) for p in a_parts)
    cast = w.dtype != BF16
    n_w = 2 if mode == "swiglu" else 1
    if n_cols is None:
        n_cols = swiglu_half if mode == "swiglu" else n_w_cols - col0
    if tm is None:
        io_bytes = jnp.dtype(out_dtype).itemsize + (4 if mode == "resid" else 0)
        tm, tn_auto = _pick_tiles(dm, k, n_cols, w.dtype.itemsize, n_w, io_bytes)
        tn = tn or tn_auto
    assert n_cols % tn == 0 and col0 % tn == 0
    bpe = n_cols // tn
    blk0 = col0 // tn

    def w_map(off):
        if a_groups:
            return lambda j, i: (e0, 0, blk0 + off + j % bpe)
        return lambda j, i: (e0 + j // bpe, 0, blk0 + off + j % bpe)

    if len(a_parts) == 1:
        lat_tiles, tail_rows = divmod(rows, tm)
        row_tiles = lat_tiles + (1 if tail_rows else 0)
        tail_rows = tail_rows or tm
        a_specs = [pl.BlockSpec((tm, k), (lambda j, i: (i, j // bpe)) if a_groups else (lambda j, i: (i, 0)))]
    else:
        lat_rows, ctx_rows = a_parts[0].shape[0], a_parts[1].shape[0]
        tail_rows = min(tm, ctx_rows)
        assert not a_groups and lat_rows % tm == 0 and ctx_rows % tail_rows == 0
        assert rows == lat_rows + ctx_rows
        lat_tiles = lat_rows // tm
        row_tiles = lat_tiles + ctx_rows // tail_rows
        a_specs = [pl.BlockSpec((tm, k), lambda j, i: (jnp.minimum(i, lat_tiles - 1), 0)),
                   pl.BlockSpec((tail_rows, k), lambda j, i: (jnp.maximum(i - lat_tiles, 0), 0))]
    in_specs = a_specs + [pl.BlockSpec((None, k, tn), w_map(0))]
    args = a_parts + [w]
    if n_w == 2:
        assert swiglu_half % tn == 0
        in_specs.append(pl.BlockSpec((None, k, tn), w_map(swiglu_half // tn)))
        args.append(w)
    in_specs += list(extra_specs)
    args += list(extra)
    scratch = [pltpu.VMEM((k, tn), BF16) for _ in range(n_w)] if cast else []
    kern = functools.partial(_mm_kernel, mode=mode, n_a=len(a_parts), lat_tiles=lat_tiles,
                             tail_rows=tail_rows, n_w=n_w, cast=cast, n_extra=len(extra),
                             gate_idx=gate_idx, scale=scale)
    return pl.pallas_call(
        kern,
        out_shape=jax.ShapeDtypeStruct((rows, n_e * n_cols), out_dtype),
        grid=(n_e * bpe, row_tiles),
        in_specs=in_specs,
        out_specs=pl.BlockSpec((tm, tn), lambda j, i: (i, j)),
        scratch_shapes=scratch,
        compiler_params=_cparams(2),
        name="matmul_" + mode,
    )(*args)


def _proj_rope(dm, h, w, layer, col0, n_cols, rows, tabs, scale=1.0, gain=None):
    tm, tn = _pick_tiles(dm, w.shape[1], n_cols, w.dtype.itemsize, 1, 2)
    tab_spec = pl.BlockSpec((tm, HEAD_DIM), lambda j, i: (i, 0))
    extra, specs = list(tabs), [tab_spec] * 3
    if gain is not None:
        extra.append(gain.reshape(1, HEAD_DIM))
        specs.append(pl.BlockSpec((1, HEAD_DIM), lambda j, i: (0, 0)))
    return _matmul(dm, h, w, rows=rows, mode="rope", out_dtype=BF16, e0=layer, col0=col0,
                   n_cols=n_cols, tm=tm, tn=tn, extra=extra, extra_specs=specs, scale=scale)


def _proj_plain(dm, h, w, layer, col0, n_cols, rows):
    return _matmul(dm, h, w, rows=rows, mode="plain", out_dtype=BF16, e0=layer, col0=col0,
                   n_cols=n_cols)


def _proj_resid(dm, a, w, layer, x, mod, gate_idx, rows):
    _, k, n = w.shape
    tm, tn = _pick_tiles(dm, k, n, w.dtype.itemsize, 1, 8)
    specs = [pl.BlockSpec((tm, tn), lambda j, i: (i, j)),
             pl.BlockSpec((None, N_MOD, tn), lambda j, i: (dm.group(i, tm), 0, j))]
    return _matmul(dm, a, w, rows=rows, mode="resid", out_dtype=F32, e0=layer, tm=tm, tn=tn,
                   extra=[x, mod], extra_specs=specs, gate_idx=gate_idx)


def _route_tables(route, counts, n_e, rows):
    tg = MOE_TILE
    expert = route[:, 0:2].astype(jnp.int32)
    rank = route[:, 2:4].astype(jnp.int32)
    cnt = counts[0, :n_e].astype(jnp.int32)
    padded = (cnt + tg - 1) // tg * tg
    ends = jnp.cumsum(padded)
    slots = (ends - padded)[expert] + rank
    assert (2 * rows) % tg == 0
    n_tiles = 2 * rows // tg + n_e
    tok = jnp.broadcast_to(jnp.arange(rows, dtype=jnp.int32)[:, None], (rows, 2))
    src = jnp.zeros((n_tiles * tg,), jnp.int32).at[slots.reshape(-1)].set(
        tok.reshape(-1), unique_indices=True)
    tile_row0 = jnp.arange(n_tiles, dtype=jnp.int32)[:, None] * tg
    tile_expert = jnp.minimum(jnp.sum(tile_row0 >= ends[None, :], axis=1), n_e - 1).astype(jnp.int32)
    return slots, src, tile_expert, n_tiles


def _rows_copy(src_hbm, dst, sem, n):
    return pltpu.make_async_copy(src_hbm.at[pl.ds(0, n), :], dst, sem)


def _gather_rows(idx_ref, stride, offset, src_hbm, dst_ref, sem):
    n = dst_ref.shape[0]
    assert n % DMA_PRIORITIES == 0

    def issue(p, carry):
        for u in range(DMA_PRIORITIES):
            r = DMA_PRIORITIES * p + u
            row = idx_ref[0, stride * r + offset]
            pltpu.make_async_copy(src_hbm.at[pl.ds(row, 1), :], dst_ref.at[pl.ds(r, 1), :],
                                  sem).start(priority=u)
        return carry

    lax.fori_loop(0, n // DMA_PRIORITIES, issue, 0, unroll=GATHER_UNROLL // DMA_PRIORITIES)


def _dispatch_kernel(cur_ref, nxt_ref, z_hbm, o_ref, buf_ref, sem):
    g = pl.program_id(0)
    slot = g % 2

    @pl.when(g == 0)
    def _():
        _gather_rows(cur_ref, 1, 0, z_hbm, buf_ref.at[0], sem.at[0])

    @pl.when(g + 1 < pl.num_programs(0))
    def _():
        _gather_rows(nxt_ref, 1, 0, z_hbm, buf_ref.at[1 - slot], sem.at[1 - slot])

    _rows_copy(z_hbm, buf_ref.at[slot], sem.at[slot], buf_ref.shape[1]).wait()
    o_ref[...] = buf_ref[slot].astype(o_ref.dtype)


def _dispatch(z, src, n_tiles):
    tg = MOE_TILE
    d = z.shape[1]
    src = src.reshape(n_tiles, 1, tg)
    return pl.pallas_call(
        _dispatch_kernel,
        out_shape=jax.ShapeDtypeStruct((n_tiles * tg, d), BF16),
        grid=(n_tiles,),
        in_specs=[pl.BlockSpec((None, 1, tg), lambda g: (g, 0, 0), memory_space=pltpu.SMEM),
                  pl.BlockSpec((None, 1, tg), lambda g: (jnp.minimum(g + 1, n_tiles - 1), 0, 0),
                               memory_space=pltpu.SMEM),
                  pl.BlockSpec(memory_space=pl.ANY)],
        out_specs=pl.BlockSpec((tg, d), lambda g: (g, 0)),
        scratch_shapes=[pltpu.VMEM((2, tg, d), z.dtype), pltpu.SemaphoreType.DMA((2,))],
        compiler_params=_cparams(1), name="moe_dispatch",
    )(src, src, z)


def _expert_changed(te_ref):
    g = pl.program_id(1)
    return (g == 0) | (te_ref[g] != te_ref[jnp.maximum(g - 1, 0)])


def _moe_up_kernel(te_ref, a_ref, wg_ref, wu_ref, o_ref, wgb_ref, wub_ref):
    @pl.when(_expert_changed(te_ref))
    def _():
        wgb_ref[...] = wg_ref[...].astype(BF16)
        wub_ref[...] = wu_ref[...].astype(BF16)

    a = a_ref[...]
    hg = jnp.dot(a, wgb_ref[...], preferred_element_type=F32)
    hu = jnp.dot(a, wub_ref[...], preferred_element_type=F32)
    o_ref[...] = (_silu(hg) * hu).astype(o_ref.dtype)


def _moe_down_kernel(te_ref, a_ref, w_ref, o_ref, wb_ref):
    @pl.when(_expert_changed(te_ref))
    def _():
        wb_ref[...] = w_ref[...].astype(BF16)

    o_ref[...] = jnp.dot(a_ref[...], wb_ref[...], preferred_element_type=F32)


def _moe_up(zg, w_in, e0, tile_expert, f):
    tg = MOE_TILE
    s, d = zg.shape
    tn = min(f, MOE_UP_COLS)
    assert f % tn == 0
    nb = f // tn
    grid_spec = pltpu.PrefetchScalarGridSpec(
        num_scalar_prefetch=1, grid=(nb, s // tg),
        in_specs=[pl.BlockSpec((tg, d), lambda j, g, te: (g, 0)),
                  pl.BlockSpec((None, d, tn), lambda j, g, te: (e0 + te[g], 0, j)),
                  pl.BlockSpec((None, d, tn), lambda j, g, te: (e0 + te[g], 0, nb + j))],
        out_specs=pl.BlockSpec((tg, tn), lambda j, g, te: (g, j)),
        scratch_shapes=[pltpu.VMEM((d, tn), BF16), pltpu.VMEM((d, tn), BF16)])
    return pl.pallas_call(
        _moe_up_kernel, out_shape=jax.ShapeDtypeStruct((s, f), BF16), grid_spec=grid_spec,
        compiler_params=_cparams(2), name="moe_up",
    )(tile_expert, zg, w_in, w_in)


def _moe_down(hdn, w_out, e0, tile_expert):
    tg = MOE_TILE
    s, f = hdn.shape
    d = w_out.shape[2]
    tn = _pick_tn(f, d, 4, 1, tg, 4)
    grid_spec = pltpu.PrefetchScalarGridSpec(
        num_scalar_prefetch=1, grid=(d // tn, s // tg),
        in_specs=[pl.BlockSpec((tg, f), lambda j, g, te: (g, 0)),
                  pl.BlockSpec((None, f, tn), lambda j, g, te: (e0 + te[g], 0, j))],
        out_specs=pl.BlockSpec((tg, tn), lambda j, g, te: (g, j)),
        scratch_shapes=[pltpu.VMEM((f, tn), BF16)])
    return pl.pallas_call(
        _moe_down_kernel, out_shape=jax.ShapeDtypeStruct((s, d), F32), grid_spec=grid_spec,
        compiler_params=_cparams(2), name="moe_down",
    )(tile_expert, hdn, w_out)


def _combine_kernel(cur_ref, nxt_ref, y_hbm, x_ref, route_ref, mod_ref, o_ref, y_ref, sem, *, gate_idx):
    i = pl.program_id(0)
    slot = i % 2
    tr = x_ref.shape[0]

    def start(idx_ref, s):
        for c in range(2):
            _gather_rows(idx_ref, 2, c, y_hbm, y_ref.at[s, c], sem.at[s, c])

    @pl.when(i == 0)
    def _():
        start(cur_ref, 0)

    @pl.when(i + 1 < pl.num_programs(0))
    def _():
        start(nxt_ref, 1 - slot)

    for c in range(2):
        _rows_copy(y_hbm, y_ref.at[slot, c], sem.at[slot, c], tr).wait()
    route = route_ref[...]
    gate = mod_ref[gate_idx:gate_idx + 1, :]
    o_ref[...] = x_ref[...] + gate * (route[:, 4:5] * y_ref[slot, 0] + route[:, 5:6] * y_ref[slot, 1])


def _moe_combine(dm, yg, slots, x, route, mod, gate_idx, rows):
    tr = COMBINE_ROW_TILE
    d = x.shape[1]
    n = rows // tr
    slots = slots.reshape(n, 1, 2 * tr)
    return pl.pallas_call(
        functools.partial(_combine_kernel, gate_idx=gate_idx),
        out_shape=jax.ShapeDtypeStruct((rows, d), F32),
        grid=(n,),
        in_specs=[pl.BlockSpec((None, 1, 2 * tr), lambda i: (i, 0, 0), memory_space=pltpu.SMEM),
                  pl.BlockSpec((None, 1, 2 * tr), lambda i: (jnp.minimum(i + 1, n - 1), 0, 0),
                               memory_space=pltpu.SMEM),
                  pl.BlockSpec(memory_space=pl.ANY),
                  pl.BlockSpec((tr, d), lambda i: (i, 0)),
                  pl.BlockSpec((tr, ROUTE_COLS), lambda i: (i, 0)),
                  pl.BlockSpec((None, N_MOD, d), lambda i: (dm.group(i, tr), 0, 0))],
        out_specs=pl.BlockSpec((tr, d), lambda i: (i, 0)),
        scratch_shapes=[pltpu.VMEM((2, 2, tr, d), F32), pltpu.SemaphoreType.DMA((2, 2))],
        compiler_params=_cparams(1), name="moe_combine",
    )(slots, slots, yg, x, route, mod)


_NT = (((1,), (1,)), ((), ()))


def _key_chunks(refs):
    out, off = [], 0
    for r in refs:
        n = r.shape[0]
        ck = min(KEY_CHUNK, n)
        for r0 in range(0, n, ck):
            out.append((r, r0, ck, off))
            off += ck
    return out


def _lane_fold(acc, v, op):
    for t in range(v.shape[1] // LANES):
        piece = v[:, t * LANES:(t + 1) * LANES]
        acc = piece if acc is None else op(acc, piece)
    return acc


def _scores_pass(q, k_refs, lo, hi, s_ref):
    mx = None
    for r, r0, ck, off in _key_chunks(k_refs):
        s = lax.dot_general(q, r[r0:r0 + ck, lo:hi], _NT, preferred_element_type=F32)
        s_ref[:, off:off + ck] = s
        mx = _lane_fold(mx, s, jnp.maximum)
    return jnp.max(mx, axis=-1, keepdims=True)


def _values_pass(s_ref, m, v_refs, lo, hi):
    acc, ls = None, None
    for r, r0, ck, off in _key_chunks(v_refs):
        e = jnp.exp2(s_ref[:, off:off + ck] - m)
        ls = _lane_fold(ls, e, jnp.add)
        pv = jnp.dot(e.astype(BF16), r[r0:r0 + ck, lo:hi], preferred_element_type=F32)
        acc = pv if acc is None else acc + pv
    return acc * (1.0 / jnp.sum(ls, axis=-1, keepdims=True))


def _diff_attn_body(q_ref, k_refs, v_refs, lam_ref, g_ref, o_ref, s_refs, sub, lam_init):
    lv = lam_ref[...]
    lam = (jnp.exp(jnp.sum(lv[0:1] * lv[1:2], axis=-1, keepdims=True))
           - jnp.exp(jnp.sum(lv[2:3] * lv[3:4], axis=-1, keepdims=True)) + lam_init)
    n_sub = q_ref.shape[0] // sub
    chains = [(t, c) for t in range(n_sub) for c in range(2)]
    m = {}
    for t, c in chains:
        lo, hi = c * HEAD_DIM, (c + 1) * HEAD_DIM
        m[t, c] = _scores_pass(q_ref[t * sub:(t + 1) * sub, lo:hi], k_refs, lo, hi, s_refs[2 * t + c])
    outs = {ch: _values_pass(s_refs[2 * ch[0] + ch[1]], m[ch], v_refs, 0, 2 * HEAD_DIM) for ch in chains}
    for t in range(n_sub):
        o = outs[t, 0] - lam * outs[t, 1]
        o = o * lax.rsqrt(jnp.mean(o * o, axis=-1, keepdims=True) + NORM_EPS) * g_ref[...]
        o_ref[t * sub:(t + 1) * sub, :] = (o * (1.0 - lam_init)).astype(o_ref.dtype)


def _gqa_attn_body(q_ref, k_refs, v_refs, o_ref, s_refs):
    m = [_scores_pass(q_ref[:, g * HEAD_DIM:(g + 1) * HEAD_DIM], k_refs, 0, HEAD_DIM, s_refs[g])
         for g in range(GQA_GROUP)]
    for g in range(GQA_GROUP):
        o = _values_pass(s_refs[g], m[g], v_refs, 0, HEAD_DIM)
        o_ref[:, g * HEAD_DIM:(g + 1) * HEAD_DIM] = o.astype(o_ref.dtype)


def _attn_kernel(q_ref, *rest, kind, n_src, n_chains, sub, lam_init):
    k_refs = rest[:n_src]
    v_refs = rest[n_src:2 * n_src]
    params = rest[2 * n_src:-n_chains - 1]
    o_ref = rest[-n_chains - 1]
    s_refs = rest[-n_chains:]
    if kind == "diff":
        _diff_attn_body(q_ref, k_refs, v_refs, params[0], params[1], o_ref, s_refs, sub, lam_init)
    else:
        _gqa_attn_body(q_ref, k_refs, v_refs, o_ref, s_refs)


def _attention_call(dm, q, k, v, *, kind, latent, params, lam_init):
    sub = dm.tq
    n_sub = DIFF_SUBTILES if (kind == "diff" and latent and dm.seq % (DIFF_SUBTILES * sub) == 0) else 1
    tq = sub * n_sub
    wq = 2 * HEAD_DIM if kind == "diff" else GQA_GROUP * HEAD_DIM
    wk = 2 * HEAD_DIM if kind == "diff" else HEAD_DIM
    n_heads = k.shape[1] // wk
    lat_per_ctx = dm.m_lat // dm.ctx
    lat_spec = pl.BlockSpec((dm.seq, wk), lambda b, h, qi: (b, h))
    ctx_spec = pl.BlockSpec((dm.ctx, wk), lambda b, h, qi: (lat_per_ctx + b, h))
    if latent:
        n_q, row0, srcs, n_keys = dm.seq // tq, 0, [lat_spec, ctx_spec], dm.seq + dm.ctx
    else:
        n_q, row0, srcs, n_keys = dm.ctx // tq, dm.m_lat // tq, [ctx_spec], dm.ctx
    q_spec = pl.BlockSpec((tq, wq), lambda b, h, qi: (row0 + b * n_q + qi, h))
    in_specs = [q_spec] + srcs + srcs + [pl.BlockSpec(p.shape, lambda b, h, qi: (0, 0)) for p in params]
    args = [q] + [k] * len(srcs) + [v] * len(srcs) + list(params)
    n_chains = (2 if kind == "diff" else GQA_GROUP) * n_sub
    return pl.pallas_call(
        functools.partial(_attn_kernel, kind=kind, n_src=len(srcs), n_chains=n_chains, sub=sub,
                          lam_init=lam_init),
        out_shape=jax.ShapeDtypeStruct((dm.b * n_q * tq, q.shape[1]), BF16),
        grid=(dm.b, n_heads, n_q),
        in_specs=in_specs,
        out_specs=pl.BlockSpec((tq, wq), lambda b, h, qi: (b * n_q + qi, h)),
        scratch_shapes=[pltpu.VMEM((sub, n_keys), F32) for _ in range(n_chains)],
        compiler_params=_cparams(3),
        name=kind + ("_attention" if latent else "_attention_ctx"),
    )(*args)


def _attention(dm, q, k, v, *, kind, ctx_out, params=(), lam_init=0.0):
    common = dict(kind=kind, params=params, lam_init=lam_init)
    y = [_attention_call(dm, q, k, v, latent=True, **common)]
    if ctx_out:
        y.append(_attention_call(dm, q, k, v, latent=False, **common))
    return y


def _dft_table_kernel(c1_ref, s1_ref, c0_ref, s0_ref, o_ref, *, sin_sign):
    c0, s0 = c0_ref[...], s0_ref[...]
    n0 = c0.shape[1]
    n = n0 * c1_ref.shape[1]
    for m1 in range(c1_ref.shape[1]):
        c1, s1 = c1_ref[:, m1:m1 + 1], s1_ref[:, m1:m1 + 1]
        o_ref[:, m1 * n0:(m1 + 1) * n0] = (c1 * c0 - s1 * s0).astype(o_ref.dtype)
        o_ref[:, n + m1 * n0:n + (m1 + 1) * n0] = (sin_sign * (s1 * c0 + c1 * s0)).astype(o_ref.dtype)


def _dft_table(n, norm, sin_sign):
    n0 = min(n, LANES)
    assert n % n0 == 0
    n1 = n // n0
    k = jnp.arange(n, dtype=jnp.int32)[:, None]

    def cs(m, scale):
        ang = ((k * m) % n).astype(F32) * (2.0 * math.pi / n)
        return jnp.cos(ang) * scale, jnp.sin(ang) * scale

    c1, s1 = cs(jnp.arange(n1, dtype=jnp.int32)[None, :] * n0, 1.0)
    c0, s0 = cs(jnp.arange(n0, dtype=jnp.int32)[None, :], norm)
    tk = min(n, DFT_ROW_TILE)
    return pl.pallas_call(
        functools.partial(_dft_table_kernel, sin_sign=sin_sign),
        out_shape=jax.ShapeDtypeStruct((n, 2 * n), BF16),
        grid=(n // tk,),
        in_specs=[pl.BlockSpec((tk, n1), lambda i: (i, 0)), pl.BlockSpec((tk, n1), lambda i: (i, 0)),
                  pl.BlockSpec((tk, n0), lambda i: (i, 0)), pl.BlockSpec((tk, n0), lambda i: (i, 0))],
        out_specs=pl.BlockSpec((tk, 2 * n), lambda i: (i, 0)),
        compiler_params=_cparams(1), name="dft_table",
    )(c1, s1, c0, s0)


def _dft_rows_kernel(c_ref, s_ref, yc_ref, ys_ref, o_ref):
    o_ref[...] = (jnp.dot(c_ref[...], yc_ref[...], preferred_element_type=F32)
                  + jnp.dot(s_ref[...], ys_ref[...], preferred_element_type=F32)).astype(o_ref.dtype)


def _dft_rows(dm, y, n, row0):
    d = dm.d
    dg = d // FOURIER_GROUPS
    table = _dft_table(n, n ** -0.5, -1.0)
    tmf = min(DFT_ROW_TILE, n)
    tn = min(512, dg)
    lb = dg // tn
    rb0 = row0 // n

    def y_map(off):
        return lambda b, j, i: (rb0 + b, (j // lb) * 2 * lb + off + j % lb)

    in_specs = [pl.BlockSpec((tmf, n), lambda b, j, i: (i, 0)),
                pl.BlockSpec((tmf, n), lambda b, j, i: (i, 1)),
                pl.BlockSpec((n, tn), y_map(0)),
                pl.BlockSpec((n, tn), y_map(lb))]
    tiles = n // tmf
    return pl.pallas_call(
        _dft_rows_kernel,
        out_shape=jax.ShapeDtypeStruct((dm.b * n, d), BF16),
        grid=(dm.b, d // tn, tiles),
        in_specs=in_specs,
        out_specs=pl.BlockSpec((tmf, tn), lambda b, j, i: (b * tiles + i, j)),
        compiler_params=_cparams(3),
        name="dft_positions",
    )(table, table, y, y)


def _fourier_mix(dm, h, ctx_out):
    d = dm.d
    dg = d // FOURIER_GROUPS
    rows = h.shape[0]
    cs = _dft_table(dg, dg ** -0.5, 1.0)[None]
    y = _matmul(dm, h, cs, rows=rows, mode="plain", out_dtype=BF16, n_e=FOURIER_GROUPS,
                a_groups=True, tn=min(2 * dg, 1024))
    f = [_dft_rows(dm, y, dm.seq, 0)]
    if ctx_out:
        f.append(_dft_rows(dm, y, dm.ctx, dm.m_lat))
    return f


def _rope_tables(dm):
    rows = dm.seq // GRID_W
    r, col = jnp.meshgrid(jnp.arange(rows), jnp.arange(GRID_W), indexing="ij")
    pos = jnp.stack([r.reshape(-1), col.reshape(-1)], axis=-1).astype(F32)
    inv_freq = 1.0 / (ROPE_THETA ** (jnp.arange(ROPE_FREQS, dtype=F32) / ROPE_FREQS))
    ang = pos[:, :, None] * inv_freq
    cos, sin = jnp.cos(ang), jnp.sin(ang)
    zero = jnp.zeros_like(sin)
    cos_t = jnp.stack([cos, cos], axis=2).reshape(dm.seq, HEAD_DIM)
    sina_t = jnp.stack([-sin, zero], axis=2).reshape(dm.seq, HEAD_DIM)
    sinb_t = jnp.stack([zero, sin], axis=2).reshape(dm.seq, HEAD_DIM)

    def full(t, fill):
        return jnp.concatenate([jnp.tile(t, (dm.b, 1)), jnp.full((dm.m_ctx, HEAD_DIM), fill, F32)], axis=0)

    return full(cos_t, 1.0), full(sina_t, 0.0), full(sinb_t, 0.0)


def kernel(x, c, ctx, c_ctx, w_mod, b_mod, norm_g, diff_w_in, diff_w_out, diff_lambda, diff_subln_g,
           fourier_w_out, gqa_w_in, gqa_w_out, gqa_qk_g, ffn_w_in, ffn_w_out, moe_router, moe_w_in,
           moe_w_out, final_g):
    dm = _Dims(x, ctx)
    b, d = dm.b, dm.d
    depth = w_mod.shape[0]
    q_scale = HEAD_DIM ** -0.5 * LOG2E

    cc = jnp.concatenate([c, c_ctx[None, :], jnp.zeros((8 - b - 1, d), F32)], axis=0)
    mods = _modulations(cc, w_mod, b_mod)[:, :b + 1].reshape(depth, b + 1, N_MOD, d)
    tabs = _rope_tables(dm)
    xs = jnp.concatenate([x.reshape(dm.m_lat, d), ctx.reshape(dm.m_ctx, d)], axis=0)

    n_moe, n_e, _, two_f = moe_w_in.shape
    moe_w_in = moe_w_in.reshape(n_moe * n_e, d, two_f)
    moe_w_out = moe_w_out.reshape(n_moe * n_e, two_f // 2, d)

    for i in range(depth):
        last = i == depth - 1
        mod = mods[i]
        rows_in = xs.shape[0]
        rows_out = dm.m_lat if last else dm.m_all
        h = _norm_mod(dm, xs, norm_g[i, 0], mod, 0, rows_in)
        kind, j = i % N_MIXERS, i // N_MIXERS
        if kind == 0:
            qk = diff_w_in.shape[2] // 3
            q = _proj_rope(dm, h, diff_w_in, j, 0, qk, rows_out, tabs, scale=q_scale)
            k = _proj_rope(dm, h, diff_w_in, j, qk, qk, rows_in, tabs)
            v = _proj_plain(dm, h, diff_w_in, j, 2 * qk, qk, rows_in)
            lam_init = 0.8 - 0.6 * math.exp(-0.3 * i)
            y = _attention(dm, q, k, v, kind="diff", ctx_out=not last,
                           params=(diff_lambda[j], diff_subln_g[j].reshape(1, 2 * HEAD_DIM)),
                           lam_init=lam_init)
            w_out = diff_w_out
        elif kind == 1:
            y = _fourier_mix(dm, h, not last)
            w_out = fourier_w_out
        else:
            kvd = (gqa_w_in.shape[2] - d) // 2
            q = _proj_rope(dm, h, gqa_w_in, j, 0, d, rows_out, tabs, scale=q_scale, gain=gqa_qk_g[j, 0])
            k = _proj_rope(dm, h, gqa_w_in, j, d, kvd, rows_in, tabs, gain=gqa_qk_g[j, 1])
            v = _proj_plain(dm, h, gqa_w_in, j, d + kvd, kvd, rows_in)
            y = _attention(dm, q, k, v, kind="gqa", ctx_out=not last)
            w_out = gqa_w_out
        xs = _proj_resid(dm, y, w_out, j, xs, mod, 2, rows_out)

        f = i // 2
        if i % 2 == 0:
            z = _norm_mod(dm, xs, norm_g[i, 1], mod, 3, rows_out)
            hdn = _matmul(dm, z, ffn_w_in, rows=rows_out, mode="swiglu", out_dtype=BF16, e0=f,
                          swiglu_half=ffn_w_in.shape[2] // 2)
            xs = _proj_resid(dm, hdn, ffn_w_out, f, xs, mod, 5, rows_out)
        else:
            z, route, counts = _norm_mod(dm, xs, norm_g[i, 1], mod, 3, rows_out, w_router=moe_router[f])
            slots, src, tile_expert, n_tiles = _route_tables(route, counts, n_e, rows_out)
            zg = _dispatch(z, src, n_tiles)
            hdn = _moe_up(zg, moe_w_in, f * n_e, tile_expert, two_f // 2)
            yg = _moe_down(hdn, moe_w_out, f * n_e, tile_expert)
            xs = _moe_combine(dm, yg, slots, xs, route, mod, 5, rows_out)

    return _final_norm(dm, xs, final_g).reshape(b, dm.seq, d)
```

```python
import functools
import math

import jax
import jax.numpy as jnp
from jax import lax
from jax.experimental import pallas as pl
from jax.experimental.pallas import tpu as pltpu

HEAD_DIM = 128
GRID_W = 64
ROPE_THETA = 10000.0
ROPE_FREQS = HEAD_DIM // 4
NORM_EPS = 1e-6
N_MOD = 6
N_MIXERS = 3
FOURIER_GROUPS = 4
GQA_GROUP = 4

LANES = 128
SUBLANES = 8
ADALN_COLS = 512
DFT_COLS = 512
VMEM_LIMIT_BYTES = 56 * 2**20
ROW_TILES = (256, 512)
DFT_ROW_TILE = 512
VMEM_TILE_BUDGET = 48 * 2**20
NORM_ROW_TILE = 256
NORM_WIDE_ROW_TILE = 512
MOE_TILE = 256
MOE_UP_COLS = 512
COMBINE_ROW_TILE = 256
GATHER_UNROLL = 8
DMA_PRIORITIES = 2
Q_TILE = 256
DIFF_SUBTILES = 2
KEY_CHUNK = 512
LOG2E = math.log2(math.e)
W_TILE_BYTES = 16 * 2**20

F32 = jnp.float32
BF16 = jnp.bfloat16


def _cparams(n_axes):
    return pltpu.CompilerParams(dimension_semantics=("arbitrary",) * n_axes,
                                vmem_limit_bytes=VMEM_LIMIT_BYTES)


def _silu(v):
    return v / (1.0 + jnp.exp(-v))


class _Dims:
    def __init__(self, x, ctx):
        self.b, self.seq, self.d = x.shape
        self.ctx = ctx.shape[1]
        self.m_lat = self.b * self.seq
        self.m_ctx = self.b * self.ctx
        self.m_all = self.m_lat + self.m_ctx
        g = math.gcd(self.seq, self.m_ctx)
        self.tm_options = sorted({math.gcd(t, self.seq) for t in ROW_TILES})
        self.tr = min(NORM_ROW_TILE, g)
        self.tr_wide = min(NORM_WIDE_ROW_TILE, g)
        self.tq = min(Q_TILE, self.ctx)
        assert self.seq % self.tq == 0 and self.ctx % self.tq == 0

    def group(self, i, tile):
        r = i * tile
        return jnp.where(r < self.m_lat, r // self.seq, self.b)


def _mod_kernel(c_ref, w_ref, b_ref, o_ref):
    s = _silu(c_ref[...]).astype(BF16)
    w = w_ref[...].astype(BF16)
    o_ref[...] = jnp.dot(s, w, preferred_element_type=F32) + b_ref[...]


def _modulations(cc, w_mod, b_mod):
    depth, d, n = w_mod.shape
    tn = ADALN_COLS if n % ADALN_COLS == 0 else n
    rows = cc.shape[0]
    return pl.pallas_call(
        _mod_kernel,
        out_shape=jax.ShapeDtypeStruct((depth, rows, n), F32),
        grid=(depth, n // tn),
        in_specs=[pl.BlockSpec((rows, d), lambda l, j: (0, 0)),
                  pl.BlockSpec((None, d, tn), lambda l, j: (l, 0, j)),
                  pl.BlockSpec((None, 1, tn), lambda l, j: (l, 0, j))],
        out_specs=pl.BlockSpec((None, rows, tn), lambda l, j: (l, 0, j)),
        compiler_params=_cparams(2),
        name="adaln_modulations",
    )(cc, w_mod, b_mod.reshape(depth, 1, n))


def _norm_mod_value(x_ref, g_ref, mod_ref, ci):
    x = x_ref[...]
    y = x * lax.rsqrt(jnp.mean(x * x, axis=-1, keepdims=True) + NORM_EPS) * g_ref[...]
    shift = mod_ref[ci:ci + 1, :]
    scale = mod_ref[ci + 1:ci + 2, :]
    return y * (1.0 + scale) + shift


def _norm_mod_kernel(x_ref, g_ref, mod_ref, o_ref, *, ci):
    o_ref[...] = _norm_mod_value(x_ref, g_ref, mod_ref, ci).astype(BF16)


def _split_bf16(v):
    hi = v.astype(BF16)
    lo = (v - hi.astype(F32)).astype(BF16)
    return hi, lo


ROUTE_COLS = 8


def _norm_mod_router_kernel(x_ref, g_ref, mod_ref, wr_ref, o_ref, route_ref, counts_ref, run_ref, *, ci, n_e):
    z = _norm_mod_value(x_ref, g_ref, mod_ref, ci)
    o_ref[...] = z
    z_hi, z_lo = _split_bf16(z)
    w_hi, w_lo = _split_bf16(wr_ref[...])
    logits = (jnp.dot(z_hi, w_hi, preferred_element_type=F32)
              + jnp.dot(z_lo, w_hi, preferred_element_type=F32)
              + jnp.dot(z_hi, w_lo, preferred_element_type=F32))
    idx = lax.broadcasted_iota(jnp.int32, logits.shape, 1).astype(F32)
    logits = jnp.where(idx < n_e, logits, -jnp.inf)
    m1 = jnp.max(logits, axis=-1, keepdims=True)
    i1 = jnp.min(jnp.where(logits == m1, idx, float(LANES)), axis=-1, keepdims=True)
    rest = jnp.where(idx == i1, -jnp.inf, logits)
    m2 = jnp.max(rest, axis=-1, keepdims=True)
    i2 = jnp.min(jnp.where(rest == m2, idx, float(LANES)), axis=-1, keepdims=True)
    e2 = jnp.exp(m2 - m1)
    w1 = 1.0 / (1.0 + e2)
    w2 = e2 * w1

    @pl.when(pl.program_id(0) == 0)
    def _():
        run_ref[...] = jnp.zeros_like(run_ref)

    sel1, sel2 = idx == i1, idx == i2
    picked = jnp.where(sel1 | sel2, 1.0, 0.0)
    rows = picked.shape[0]
    earlier = (lax.broadcasted_iota(jnp.int32, (rows, rows), 0)
               > lax.broadcasted_iota(jnp.int32, (rows, rows), 1))
    before = jnp.dot(jnp.where(earlier, 1.0, 0.0).astype(BF16), picked.astype(BF16),
                     preferred_element_type=F32) + run_ref[...]
    rank1 = jnp.sum(jnp.where(sel1, before, 0.0), axis=-1, keepdims=True)
    rank2 = jnp.sum(jnp.where(sel2, before, 0.0), axis=-1, keepdims=True)
    run_ref[...] = run_ref[...] + jnp.sum(picked, axis=0, keepdims=True)
    counts_ref[...] = run_ref[...]

    route = jnp.zeros_like(logits)
    for col, val in enumerate((i1, i2, rank1, rank2, w1, w2)):
        route = jnp.where(idx == col, val, route)
    route_ref[...] = route[:, :ROUTE_COLS]


def _norm_mod(dm, x, g, mod, ci, rows, w_router=None):
    d = x.shape[1]
    tr = dm.tr if w_router is not None else dm.tr_wide
    in_specs = [pl.BlockSpec((tr, d), lambda i: (i, 0)),
                pl.BlockSpec((1, d), lambda i: (0, 0)),
                pl.BlockSpec((None, N_MOD, d), lambda i: (dm.group(i, tr), 0, 0))]
    h_spec = pl.BlockSpec((tr, d), lambda i: (i, 0))
    h_shape = jax.ShapeDtypeStruct((rows, d), BF16)
    if w_router is None:
        return pl.pallas_call(
            functools.partial(_norm_mod_kernel, ci=ci),
            out_shape=h_shape, grid=(rows // tr,), in_specs=in_specs, out_specs=h_spec,
            compiler_params=_cparams(1), name="norm_modulate",
        )(x, g.reshape(1, d), mod)
    n_e = w_router.shape[1]
    assert n_e <= LANES
    w_router = jnp.pad(w_router, ((0, 0), (0, LANES - n_e)))
    return pl.pallas_call(
        functools.partial(_norm_mod_router_kernel, ci=ci, n_e=n_e),
        out_shape=(jax.ShapeDtypeStruct((rows, d), F32),
                   jax.ShapeDtypeStruct((rows, ROUTE_COLS), F32),
                   jax.ShapeDtypeStruct((1, LANES), F32)),
        grid=(rows // tr,),
        in_specs=in_specs + [pl.BlockSpec((d, LANES), lambda i: (0, 0))],
        out_specs=(h_spec, pl.BlockSpec((tr, ROUTE_COLS), lambda i: (i, 0)),
                   pl.BlockSpec((1, LANES), lambda i: (0, 0))),
        scratch_shapes=[pltpu.VMEM((1, LANES), F32)],
        compiler_params=_cparams(1), name="norm_modulate_router",
    )(x, g.reshape(1, d), mod, w_router)


def _final_norm_kernel(x_ref, g_ref, o_ref):
    x = x_ref[...]
    o_ref[...] = x * lax.rsqrt(jnp.mean(x * x, axis=-1, keepdims=True) + NORM_EPS) * g_ref[...]


def _final_norm(dm, x, g):
    rows, d = x.shape
    tr = dm.tr
    return pl.pallas_call(
        _final_norm_kernel,
        out_shape=jax.ShapeDtypeStruct((rows, d), F32),
        grid=(rows // tr,),
        in_specs=[pl.BlockSpec((tr, d), lambda i: (i, 0)), pl.BlockSpec((1, d), lambda i: (0, 0))],
        out_specs=pl.BlockSpec((tr, d), lambda i: (i, 0)),
        compiler_params=_cparams(1), name="final_norm",
    )(x, g.reshape(1, d))


def _rope_epilogue(acc, cos_ref, sina_ref, sinb_ref, g_ref, o_ref, scale):
    cos, sina, sinb = cos_ref[...], sina_ref[...], sinb_ref[...]
    for c in range(acc.shape[1] // HEAD_DIM):
        xh = acc[:, c * HEAD_DIM:(c + 1) * HEAD_DIM]
        if g_ref is not None:
            xh = xh * lax.rsqrt(jnp.mean(xh * xh, axis=-1, keepdims=True) + NORM_EPS) * g_ref[...]
        fwd = pltpu.roll(xh, HEAD_DIM - ROPE_FREQS, 1)
        bwd = pltpu.roll(xh, ROPE_FREQS, 1)
        r = xh * cos + fwd * sina + bwd * sinb
        if scale != 1.0:
            r = r * scale
        o_ref[:, c * HEAD_DIM:(c + 1) * HEAD_DIM] = r.astype(o_ref.dtype)


def _mm_kernel(*refs, mode, n_a, lat_tiles, tail_rows, n_w, cast, n_extra, gate_idx, scale):
    a_refs = refs[:n_a]
    w_refs = refs[n_a:n_a + n_w]
    extra = refs[n_a + n_w:n_a + n_w + n_extra]
    o_ref = refs[n_a + n_w + n_extra]
    wb_refs = refs[n_a + n_w + n_extra + 1:]

    if cast:
        @pl.when(pl.program_id(1) == 0)
        def _():
            for w_ref, wb_ref in zip(w_refs, wb_refs):
                wb_ref[...] = w_ref[...].astype(BF16)
        w_srcs = wb_refs
    else:
        w_srcs = w_refs

    tm = o_ref.shape[0]
    if n_a == 1 and tail_rows == tm:
        _mm_tile(a_refs[0], tm, w_srcs, extra, o_ref, mode, gate_idx, scale)
        return

    i = pl.program_id(1)

    @pl.when(i < lat_tiles)
    def _():
        _mm_tile(a_refs[0], tm, w_srcs, extra, o_ref, mode, gate_idx, scale)

    @pl.when(i >= lat_tiles)
    def _():
        _mm_tile(a_refs[-1], tail_rows, w_srcs, extra, o_ref, mode, gate_idx, scale)


def _mm_tile(a_ref, nrows, w_srcs, extra, o_ref, mode, gate_idx, scale):
    rows = lambda ref: ref.at[0:nrows]
    a = a_ref[0:nrows, :]
    accs = [jnp.dot(a, w[...], preferred_element_type=F32) for w in w_srcs]
    n_extra = len(extra)
    o_ref = rows(o_ref)

    if mode == "plain":
        o_ref[...] = accs[0].astype(o_ref.dtype)
    elif mode == "rope":
        g_ref = extra[3] if n_extra == 4 else None
        _rope_epilogue(accs[0], rows(extra[0]), rows(extra[1]), rows(extra[2]), g_ref, o_ref, scale)
    elif mode == "resid":
        x_ref, mod_ref = extra
        gate = mod_ref[gate_idx:gate_idx + 1, :]
        o_ref[...] = rows(x_ref)[...] + gate * accs[0]
    elif mode == "swiglu":
        o_ref[...] = (_silu(accs[0]) * accs[1]).astype(o_ref.dtype)
    else:
        raise ValueError(mode)


def _pick_tn(k, n, w_itemsize, n_w, tm, io_bytes):
    best = LANES if n % LANES == 0 else n
    for tn in range(LANES, n + 1, LANES):
        w_blocks = n_w * k * tn * w_itemsize
        need = 2 * tm * k * 2 + 2 * w_blocks + (n_w * k * tn * 2 if w_itemsize == 4 else 0) + 2 * tm * tn * io_bytes
        if n % tn == 0 and w_blocks <= W_TILE_BYTES and need <= VMEM_TILE_BUDGET:
            best = tn
    return best


def _pick_tiles(dm, k, n, w_itemsize, n_w, io_bytes):
    options = [(_pick_tn(k, n, w_itemsize, n_w, tm, io_bytes), tm) for tm in dm.tm_options]
    tn, tm = max(options)
    return tm, tn


def _matmul(dm, a, w, *, rows, mode, out_dtype, e0=0, n_e=1, col0=0, n_cols=None, tm=None, tn=None,
            extra=(), extra_specs=(), gate_idx=0, scale=1.0, swiglu_half=None, a_groups=False):
    a_parts = list(a) if isinstance(a, (list, tuple)) else [a]
    _, k, n_w_cols = w.shape
    assert all(p.shape[1] == (n_e * k if a_groups else k) for p in a_parts)
    cast = w.dtype != BF16
    n_w = 2 if mode == "swiglu" else 1
    if n_cols is None:
        n_cols = swiglu_half if mode == "swiglu" else n_w_cols - col0
    if tm is None:
        io_bytes = jnp.dtype(out_dtype).itemsize + (4 if mode == "resid" else 0)
        tm, tn_auto = _pick_tiles(dm, k, n_cols, w.dtype.itemsize, n_w, io_bytes)
        tn = tn or tn_auto
    assert n_cols % tn == 0 and col0 % tn == 0
    bpe = n_cols // tn
    blk0 = col0 // tn

    def w_map(off):
        if a_groups:
            return lambda j, i: (e0, 0, blk0 + off + j % bpe)
        return lambda j, i: (e0 + j // bpe, 0, blk0 + off + j % bpe)

    if len(a_parts) == 1:
        lat_tiles, tail_rows = divmod(rows, tm)
        row_tiles = lat_tiles + (1 if tail_rows else 0)
        tail_rows = tail_rows or tm
        a_specs = [pl.BlockSpec((tm, k), (lambda j, i: (i, j // bpe)) if a_groups else (lambda j, i: (i, 0)))]
    else:
        lat_rows, ctx_rows = a_parts[0].shape[0], a_parts[1].shape[0]
        tail_rows = min(tm, ctx_rows)
        assert not a_groups and lat_rows % tm == 0 and ctx_rows % tail_rows == 0
        assert rows == lat_rows + ctx_rows
        lat_tiles = lat_rows // tm
        row_tiles = lat_tiles + ctx_rows // tail_rows
        a_specs = [pl.BlockSpec((tm, k), lambda j, i: (jnp.minimum(i, lat_tiles - 1), 0)),
                   pl.BlockSpec((tail_rows, k), lambda j, i: (jnp.maximum(i - lat_tiles, 0), 0))]
    in_specs = a_specs + [pl.BlockSpec((None, k, tn), w_map(0))]
    args = a_parts + [w]
    if n_w == 2:
        assert swiglu_half % tn == 0
        in_specs.append(pl.BlockSpec((None, k, tn), w_map(swiglu_half // tn)))
        args.append(w)
    in_specs += list(extra_specs)
    args += list(extra)
    scratch = [pltpu.VMEM((k, tn), BF16) for _ in range(n_w)] if cast else []
    kern = functools.partial(_mm_kernel, mode=mode, n_a=len(a_parts), lat_tiles=lat_tiles,
                             tail_rows=tail_rows, n_w=n_w, cast=cast, n_extra=len(extra),
                             gate_idx=gate_idx, scale=scale)
    return pl.pallas_call(
        kern,
        out_shape=jax.ShapeDtypeStruct((rows, n_e * n_cols), out_dtype),
        grid=(n_e * bpe, row_tiles),
        in_specs=in_specs,
        out_specs=pl.BlockSpec((tm, tn), lambda j, i: (i, j)),
        scratch_shapes=scratch,
        compiler_params=_cparams(2),
        name="matmul_" + mode,
    )(*args)


def _proj_rope(dm, h, w, layer, col0, n_cols, rows, tabs, scale=1.0, gain=None):
    tm, tn = _pick_tiles(dm, w.shape[1], n_cols, w.dtype.itemsize, 1, 2)
    tab_spec = pl.BlockSpec((tm, HEAD_DIM), lambda j, i: (i, 0))
    extra, specs = list(tabs), [tab_spec] * 3
    if gain is not None:
        extra.append(gain.reshape(1, HEAD_DIM))
        specs.append(pl.BlockSpec((1, HEAD_DIM), lambda j, i: (0, 0)))
    return _matmul(dm, h, w, rows=rows, mode="rope", out_dtype=BF16, e0=layer, col0=col0,
                   n_cols=n_cols, tm=tm, tn=tn, extra=extra, extra_specs=specs, scale=scale)


def _proj_plain(dm, h, w, layer, col0, n_cols, rows):
    return _matmul(dm, h, w, rows=rows, mode="plain", out_dtype=BF16, e0=layer, col0=col0,
                   n_cols=n_cols)


def _proj_resid(dm, a, w, layer, x, mod, gate_idx, rows):
    _, k, n = w.shape
    tm, tn = _pick_tiles(dm, k, n, w.dtype.itemsize, 1, 8)
    specs = [pl.BlockSpec((tm, tn), lambda j, i: (i, j)),
             pl.BlockSpec((None, N_MOD, tn), lambda j, i: (dm.group(i, tm), 0, j))]
    return _matmul(dm, a, w, rows=rows, mode="resid", out_dtype=F32, e0=layer, tm=tm, tn=tn,
                   extra=[x, mod], extra_specs=specs, gate_idx=gate_idx)


def _route_tables(route, counts, n_e, rows):
    tg = MOE_TILE
    expert = route[:, 0:2].astype(jnp.int32)
    rank = route[:, 2:4].astype(jnp.int32)
    cnt = counts[0, :n_e].astype(jnp.int32)
    padded = (cnt + tg - 1) // tg * tg
    ends = jnp.cumsum(padded)
    slots = (ends - padded)[expert] + rank
    assert (2 * rows) % tg == 0
    n_tiles = 2 * rows // tg + n_e
    tok = jnp.broadcast_to(jnp.arange(rows, dtype=jnp.int32)[:, None], (rows, 2))
    src = jnp.zeros((n_tiles * tg,), jnp.int32).at[slots.reshape(-1)].set(
        tok.reshape(-1), unique_indices=True)
    tile_row0 = jnp.arange(n_tiles, dtype=jnp.int32)[:, None] * tg
    tile_expert = jnp.minimum(jnp.sum(tile_row0 >= ends[None, :], axis=1), n_e - 1).astype(jnp.int32)
    return slots, src, tile_expert, n_tiles


def _rows_copy(src_hbm, dst, sem, n):
    return pltpu.make_async_copy(src_hbm.at[pl.ds(0, n), :], dst, sem)


def _gather_rows(idx_ref, stride, offset, src_hbm, dst_ref, sem):
    n = dst_ref.shape[0]
    assert n % DMA_PRIORITIES == 0

    def issue(p, carry):
        for u in range(DMA_PRIORITIES):
            r = DMA_PRIORITIES * p + u
            row = idx_ref[0, stride * r + offset]
            pltpu.make_async_copy(src_hbm.at[pl.ds(row, 1), :], dst_ref.at[pl.ds(r, 1), :],
                                  sem).start(priority=u)
        return carry

    lax.fori_loop(0, n // DMA_PRIORITIES, issue, 0, unroll=GATHER_UNROLL // DMA_PRIORITIES)


def _dispatch_kernel(cur_ref, nxt_ref, z_hbm, o_ref, buf_ref, sem):
    g = pl.program_id(0)
    slot = g % 2

    @pl.when(g == 0)
    def _():
        _gather_rows(cur_ref, 1, 0, z_hbm, buf_ref.at[0], sem.at[0])

    @pl.when(g + 1 < pl.num_programs(0))
    def _():
        _gather_rows(nxt_ref, 1, 0, z_hbm, buf_ref.at[1 - slot], sem.at[1 - slot])

    _rows_copy(z_hbm, buf_ref.at[slot], sem.at[slot], buf_ref.shape[1]).wait()
    o_ref[...] = buf_ref[slot].astype(o_ref.dtype)


def _dispatch(z, src, n_tiles):
    tg = MOE_TILE
    d = z.shape[1]
    src = src.reshape(n_tiles, 1, tg)
    return pl.pallas_call(
        _dispatch_kernel,
        out_shape=jax.ShapeDtypeStruct((n_tiles * tg, d), BF16),
        grid=(n_tiles,),
        in_specs=[pl.BlockSpec((None, 1, tg), lambda g: (g, 0, 0), memory_space=pltpu.SMEM),
                  pl.BlockSpec((None, 1, tg), lambda g: (jnp.minimum(g + 1, n_tiles - 1), 0, 0),
                               memory_space=pltpu.SMEM),
                  pl.BlockSpec(memory_space=pl.ANY)],
        out_specs=pl.BlockSpec((tg, d), lambda g: (g, 0)),
        scratch_shapes=[pltpu.VMEM((2, tg, d), z.dtype), pltpu.SemaphoreType.DMA((2,))],
        compiler_params=_cparams(1), name="moe_dispatch",
    )(src, src, z)


def _expert_changed(te_ref):
    g = pl.program_id(1)
    return (g == 0) | (te_ref[g] != te_ref[jnp.maximum(g - 1, 0)])


def _moe_up_kernel(te_ref, a_ref, wg_ref, wu_ref, o_ref, wgb_ref, wub_ref):
    @pl.when(_expert_changed(te_ref))
    def _():
        wgb_ref[...] = wg_ref[...].astype(BF16)
        wub_ref[...] = wu_ref[...].astype(BF16)

    a = a_ref[...]
    hg = jnp.dot(a, wgb_ref[...], preferred_element_type=F32)
    hu = jnp.dot(a, wub_ref[...], preferred_element_type=F32)
    o_ref[...] = (_silu(hg) * hu).astype(o_ref.dtype)


def _moe_down_kernel(te_ref, a_ref, w_ref, o_ref, wb_ref):
    @pl.when(_expert_changed(te_ref))
    def _():
        wb_ref[...] = w_ref[...].astype(BF16)

    o_ref[...] = jnp.dot(a_ref[...], wb_ref[...], preferred_element_type=F32)


def _moe_up(zg, w_in, e0, tile_expert, f):
    tg = MOE_TILE
    s, d = zg.shape
    tn = min(f, MOE_UP_COLS)
    assert f % tn == 0
    nb = f // tn
    grid_spec = pltpu.PrefetchScalarGridSpec(
        num_scalar_prefetch=1, grid=(nb, s // tg),
        in_specs=[pl.BlockSpec((tg, d), lambda j, g, te: (g, 0)),
                  pl.BlockSpec((None, d, tn), lambda j, g, te: (e0 + te[g], 0, j)),
                  pl.BlockSpec((None, d, tn), lambda j, g, te: (e0 + te[g], 0, nb + j))],
        out_specs=pl.BlockSpec((tg, tn), lambda j, g, te: (g, j)),
        scratch_shapes=[pltpu.VMEM((d, tn), BF16), pltpu.VMEM((d, tn), BF16)])
    return pl.pallas_call(
        _moe_up_kernel, out_shape=jax.ShapeDtypeStruct((s, f), BF16), grid_spec=grid_spec,
        compiler_params=_cparams(2), name="moe_up",
    )(tile_expert, zg, w_in, w_in)


def _moe_down(hdn, w_out, e0, tile_expert):
    tg = MOE_TILE
    s, f = hdn.shape
    d = w_out.shape[2]
    tn = _pick_tn(f, d, 4, 1, tg, 4)
    grid_spec = pltpu.PrefetchScalarGridSpec(
        num_scalar_prefetch=1, grid=(d // tn, s // tg),
        in_specs=[pl.BlockSpec((tg, f), lambda j, g, te: (g, 0)),
                  pl.BlockSpec((None, f, tn), lambda j, g, te: (e0 + te[g], 0, j))],
        out_specs=pl.BlockSpec((tg, tn), lambda j, g, te: (g, j)),
        scratch_shapes=[pltpu.VMEM((f, tn), BF16)])
    return pl.pallas_call(
        _moe_down_kernel, out_shape=jax.ShapeDtypeStruct((s, d), F32), grid_spec=grid_spec,
        compiler_params=_cparams(2), name="moe_down",
    )(tile_expert, hdn, w_out)


def _combine_kernel(cur_ref, nxt_ref, y_hbm, x_ref, route_ref, mod_ref, *rest, gate_idx, final_norm):
    o_ref, y_ref, sem = rest[-3:]
    i = pl.program_id(0)
    slot = i % 2
    tr = x_ref.shape[0]

    def start(idx_ref, s):
        for c in range(2):
            _gather_rows(idx_ref, 2, c, y_hbm, y_ref.at[s, c], sem.at[s, c])

    @pl.when(i == 0)
    def _():
        start(cur_ref, 0)

    @pl.when(i + 1 < pl.num_programs(0))
    def _():
        start(nxt_ref, 1 - slot)

    for c in range(2):
        _rows_copy(y_hbm, y_ref.at[slot, c], sem.at[slot, c], tr).wait()
    route = route_ref[...]
    gate = mod_ref[gate_idx:gate_idx + 1, :]
    x = x_ref[...] + gate * (route[:, 4:5] * y_ref[slot, 0] + route[:, 5:6] * y_ref[slot, 1])
    if final_norm:
        x = x * lax.rsqrt(jnp.mean(x * x, axis=-1, keepdims=True) + NORM_EPS) * rest[0][...]
    o_ref[...] = x


def _moe_combine(dm, yg, slots, x, route, mod, gate_idx, rows, final_g=None):
    tr = COMBINE_ROW_TILE
    d = x.shape[1]
    n = rows // tr
    slots = slots.reshape(n, 1, 2 * tr)
    in_specs = [pl.BlockSpec((None, 1, 2 * tr), lambda i: (i, 0, 0), memory_space=pltpu.SMEM),
                pl.BlockSpec((None, 1, 2 * tr), lambda i: (jnp.minimum(i + 1, n - 1), 0, 0),
                             memory_space=pltpu.SMEM),
                pl.BlockSpec(memory_space=pl.ANY),
                pl.BlockSpec((tr, d), lambda i: (i, 0)),
                pl.BlockSpec((tr, ROUTE_COLS), lambda i: (i, 0)),
                pl.BlockSpec((None, N_MOD, d), lambda i: (dm.group(i, tr), 0, 0))]
    args = [slots, slots, yg, x, route, mod]
    if final_g is not None:
        in_specs.append(pl.BlockSpec((1, d), lambda i: (0, 0)))
        args.append(final_g.reshape(1, d))
    return pl.pallas_call(
        functools.partial(_combine_kernel, gate_idx=gate_idx, final_norm=final_g is not None),
        out_shape=jax.ShapeDtypeStruct((rows, d), F32),
        grid=(n,),
        in_specs=in_specs,
        out_specs=pl.BlockSpec((tr, d), lambda i: (i, 0)),
        scratch_shapes=[pltpu.VMEM((2, 2, tr, d), F32), pltpu.SemaphoreType.DMA((2, 2))],
        compiler_params=_cparams(1), name="moe_combine",
    )(*args)


_NT = (((1,), (1,)), ((), ()))


def _key_chunks(refs):
    out, off = [], 0
    for r in refs:
        n = r.shape[0]
        ck = min(KEY_CHUNK, n)
        for r0 in range(0, n, ck):
            out.append((r, r0, ck, off))
            off += ck
    return out


def _lane_fold(acc, v, op):
    for t in range(v.shape[1] // LANES):
        piece = v[:, t * LANES:(t + 1) * LANES]
        acc = piece if acc is None else op(acc, piece)
    return acc


def _scores_pass(q, k_refs, lo, hi, s_ref):
    mx = None
    for r, r0, ck, off in _key_chunks(k_refs):
        s = lax.dot_general(q, r[r0:r0 + ck, lo:hi], _NT, preferred_element_type=F32)
        s_ref[:, off:off + ck] = s
        mx = _lane_fold(mx, s, jnp.maximum)
    return jnp.max(mx, axis=-1, keepdims=True)


def _values_pass(s_ref, m, v_refs, lo, hi):
    acc, ls = None, None
    for r, r0, ck, off in _key_chunks(v_refs):
        e = jnp.exp2(s_ref[:, off:off + ck] - m)
        ls = _lane_fold(ls, e, jnp.add)
        pv = jnp.dot(e.astype(BF16), r[r0:r0 + ck, lo:hi], preferred_element_type=F32)
        acc = pv if acc is None else acc + pv
    return acc * (1.0 / jnp.sum(ls, axis=-1, keepdims=True))


def _diff_attn_body(q_ref, k_refs, v_refs, lam_ref, g_ref, o_ref, s_refs, sub, lam_init):
    lv = lam_ref[...]
    lam = (jnp.exp(jnp.sum(lv[0:1] * lv[1:2], axis=-1, keepdims=True))
           - jnp.exp(jnp.sum(lv[2:3] * lv[3:4], axis=-1, keepdims=True)) + lam_init)
    n_sub = q_ref.shape[0] // sub
    chains = [(t, c) for t in range(n_sub) for c in range(2)]
    m = {}
    for t, c in chains:
        lo, hi = c * HEAD_DIM, (c + 1) * HEAD_DIM
        m[t, c] = _scores_pass(q_ref[t * sub:(t + 1) * sub, lo:hi], k_refs, lo, hi, s_refs[2 * t + c])
    outs = {ch: _values_pass(s_refs[2 * ch[0] + ch[1]], m[ch], v_refs, 0, 2 * HEAD_DIM) for ch in chains}
    for t in range(n_sub):
        o = outs[t, 0] - lam * outs[t, 1]
        o = o * lax.rsqrt(jnp.mean(o * o, axis=-1, keepdims=True) + NORM_EPS) * g_ref[...]
        o_ref[t * sub:(t + 1) * sub, :] = (o * (1.0 - lam_init)).astype(o_ref.dtype)


def _gqa_attn_body(q_ref, k_refs, v_refs, o_ref, s_refs):
    m = [_scores_pass(q_ref[:, g * HEAD_DIM:(g + 1) * HEAD_DIM], k_refs, 0, HEAD_DIM, s_refs[g])
         for g in range(GQA_GROUP)]
    for g in range(GQA_GROUP):
        o = _values_pass(s_refs[g], m[g], v_refs, 0, HEAD_DIM)
        o_ref[:, g * HEAD_DIM:(g + 1) * HEAD_DIM] = o.astype(o_ref.dtype)


def _attn_kernel(q_ref, *rest, kind, n_src, n_chains, sub, lam_init):
    k_refs = rest[:n_src]
    v_refs = rest[n_src:2 * n_src]
    params = rest[2 * n_src:-n_chains - 1]
    o_ref = rest[-n_chains - 1]
    s_refs = rest[-n_chains:]
    if kind == "diff":
        _diff_attn_body(q_ref, k_refs, v_refs, params[0], params[1], o_ref, s_refs, sub, lam_init)
    else:
        _gqa_attn_body(q_ref, k_refs, v_refs, o_ref, s_refs)


def _attention_call(dm, q, k, v, *, kind, latent, params, lam_init):
    sub = dm.tq
    n_sub = DIFF_SUBTILES if (kind == "diff" and latent and dm.seq % (DIFF_SUBTILES * sub) == 0) else 1
    tq = sub * n_sub
    wq = 2 * HEAD_DIM if kind == "diff" else GQA_GROUP * HEAD_DIM
    wk = 2 * HEAD_DIM if kind == "diff" else HEAD_DIM
    n_heads = k.shape[1] // wk
    lat_per_ctx = dm.m_lat // dm.ctx
    lat_spec = pl.BlockSpec((dm.seq, wk), lambda b, h, qi: (b, h))
    ctx_spec = pl.BlockSpec((dm.ctx, wk), lambda b, h, qi: (lat_per_ctx + b, h))
    if latent:
        n_q, row0, srcs, n_keys = dm.seq // tq, 0, [lat_spec, ctx_spec], dm.seq + dm.ctx
    else:
        n_q, row0, srcs, n_keys = dm.ctx // tq, dm.m_lat // tq, [ctx_spec], dm.ctx
    q_spec = pl.BlockSpec((tq, wq), lambda b, h, qi: (row0 + b * n_q + qi, h))
    in_specs = [q_spec] + srcs + srcs + [pl.BlockSpec(p.shape, lambda b, h, qi: (0, 0)) for p in params]
    args = [q] + [k] * len(srcs) + [v] * len(srcs) + list(params)
    n_chains = (2 if kind == "diff" else GQA_GROUP) * n_sub
    return pl.pallas_call(
        functools.partial(_attn_kernel, kind=kind, n_src=len(srcs), n_chains=n_chains, sub=sub,
                          lam_init=lam_init),
        out_shape=jax.ShapeDtypeStruct((dm.b * n_q * tq, q.shape[1]), BF16),
        grid=(dm.b, n_heads, n_q),
        in_specs=in_specs,
        out_specs=pl.BlockSpec((tq, wq), lambda b, h, qi: (b * n_q + qi, h)),
        scratch_shapes=[pltpu.VMEM((sub, n_keys), F32) for _ in range(n_chains)],
        compiler_params=_cparams(3),
        name=kind + ("_attention" if latent else "_attention_ctx"),
    )(*args)


def _attention(dm, q, k, v, *, kind, ctx_out, params=(), lam_init=0.0):
    common = dict(kind=kind, params=params, lam_init=lam_init)
    y = [_attention_call(dm, q, k, v, latent=True, **common)]
    if ctx_out:
        y.append(_attention_call(dm, q, k, v, latent=False, **common))
    return y


def _dft_table_kernel(c1_ref, s1_ref, c0_ref, s0_ref, o_ref, *, sin_sign):
    c0, s0 = c0_ref[...], s0_ref[...]
    n0 = c0.shape[1]
    n = n0 * c1_ref.shape[1]
    for m1 in range(c1_ref.shape[1]):
        c1, s1 = c1_ref[:, m1:m1 + 1], s1_ref[:, m1:m1 + 1]
        o_ref[:, m1 * n0:(m1 + 1) * n0] = (c1 * c0 - s1 * s0).astype(o_ref.dtype)
        o_ref[:, n + m1 * n0:n + (m1 + 1) * n0] = (sin_sign * (s1 * c0 + c1 * s0)).astype(o_ref.dtype)


def _dft_table(n, norm, sin_sign):
    n0 = min(n, LANES)
    assert n % n0 == 0
    n1 = n // n0
    k = jnp.arange(n, dtype=jnp.int32)[:, None]

    def cs(m, scale):
        ang = ((k * m) % n).astype(F32) * (2.0 * math.pi / n)
        return jnp.cos(ang) * scale, jnp.sin(ang) * scale

    c1, s1 = cs(jnp.arange(n1, dtype=jnp.int32)[None, :] * n0, 1.0)
    c0, s0 = cs(jnp.arange(n0, dtype=jnp.int32)[None, :], norm)
    tk = min(n, DFT_ROW_TILE)
    return pl.pallas_call(
        functools.partial(_dft_table_kernel, sin_sign=sin_sign),
        out_shape=jax.ShapeDtypeStruct((n, 2 * n), BF16),
        grid=(n // tk,),
        in_specs=[pl.BlockSpec((tk, n1), lambda i: (i, 0)), pl.BlockSpec((tk, n1), lambda i: (i, 0)),
                  pl.BlockSpec((tk, n0), lambda i: (i, 0)), pl.BlockSpec((tk, n0), lambda i: (i, 0))],
        out_specs=pl.BlockSpec((tk, 2 * n), lambda i: (i, 0)),
        compiler_params=_cparams(1), name="dft_table",
    )(c1, s1, c0, s0)


def _dft_rows_kernel(c_ref, s_ref, yc_ref, ys_ref, o_ref):
    o_ref[...] = (jnp.dot(c_ref[...], yc_ref[...], preferred_element_type=F32)
                  + jnp.dot(s_ref[...], ys_ref[...], preferred_element_type=F32)).astype(o_ref.dtype)


def _dft_rows(dm, y, n, row0):
    d = dm.d
    dg = d // FOURIER_GROUPS
    table = _dft_table(n, n ** -0.5, -1.0)
    tmf = min(DFT_ROW_TILE, n)
    tn = min(DFT_COLS, dg)
    lb = dg // tn
    rb0 = row0 // n

    def y_map(off):
        return lambda b, j, i: (rb0 + b, (j // lb) * 2 * lb + off + j % lb)

    in_specs = [pl.BlockSpec((tmf, n), lambda b, j, i: (i, 0)),
                pl.BlockSpec((tmf, n), lambda b, j, i: (i, 1)),
                pl.BlockSpec((n, tn), y_map(0)),
                pl.BlockSpec((n, tn), y_map(lb))]
    tiles = n // tmf
    return pl.pallas_call(
        _dft_rows_kernel,
        out_shape=jax.ShapeDtypeStruct((dm.b * n, d), BF16),
        grid=(dm.b, d // tn, tiles),
        in_specs=in_specs,
        out_specs=pl.BlockSpec((tmf, tn), lambda b, j, i: (b * tiles + i, j)),
        compiler_params=_cparams(3),
        name="dft_positions",
    )(table, table, y, y)


def _fourier_mix(dm, h, ctx_out):
    d = dm.d
    dg = d // FOURIER_GROUPS
    rows = h.shape[0]
    cs = _dft_table(dg, dg ** -0.5, 1.0)[None]
    y = _matmul(dm, h, cs, rows=rows, mode="plain", out_dtype=BF16, n_e=FOURIER_GROUPS,
                a_groups=True, tn=min(2 * dg, 1024))
    f = [_dft_rows(dm, y, dm.seq, 0)]
    if ctx_out:
        f.append(_dft_rows(dm, y, dm.ctx, dm.m_lat))
    return f


def _rope_tables(dm):
    rows = dm.seq // GRID_W
    r, col = jnp.meshgrid(jnp.arange(rows), jnp.arange(GRID_W), indexing="ij")
    pos = jnp.stack([r.reshape(-1), col.reshape(-1)], axis=-1).astype(F32)
    inv_freq = 1.0 / (ROPE_THETA ** (jnp.arange(ROPE_FREQS, dtype=F32) / ROPE_FREQS))
    ang = pos[:, :, None] * inv_freq
    cos, sin = jnp.cos(ang), jnp.sin(ang)
    zero = jnp.zeros_like(sin)
    cos_t = jnp.stack([cos, cos], axis=2).reshape(dm.seq, HEAD_DIM)
    sina_t = jnp.stack([-sin, zero], axis=2).reshape(dm.seq, HEAD_DIM)
    sinb_t = jnp.stack([zero, sin], axis=2).reshape(dm.seq, HEAD_DIM)

    def full(t, fill):
        return jnp.concatenate([jnp.tile(t, (dm.b, 1)), jnp.full((dm.m_ctx, HEAD_DIM), fill, F32)], axis=0)

    return full(cos_t, 1.0), full(sina_t, 0.0), full(sinb_t, 0.0)


def kernel(x, c, ctx, c_ctx, w_mod, b_mod, norm_g, diff_w_in, diff_w_out, diff_lambda, diff_subln_g,
           fourier_w_out, gqa_w_in, gqa_w_out, gqa_qk_g, ffn_w_in, ffn_w_out, moe_router, moe_w_in,
           moe_w_out, final_g):
    dm = _Dims(x, ctx)
    b, d = dm.b, dm.d
    depth = w_mod.shape[0]
    q_scale = HEAD_DIM ** -0.5 * LOG2E

    cond_rows = -(-(b + 1) // SUBLANES) * SUBLANES
    cc = jnp.concatenate([c, c_ctx[None, :], jnp.zeros((cond_rows - b - 1, d), F32)], axis=0)
    mods = _modulations(cc, w_mod, b_mod)[:, :b + 1].reshape(depth, b + 1, N_MOD, d)
    tabs = _rope_tables(dm)
    xs = jnp.concatenate([x.reshape(dm.m_lat, d), ctx.reshape(dm.m_ctx, d)], axis=0)

    n_moe, n_e, _, two_f = moe_w_in.shape
    moe_w_in = moe_w_in.reshape(n_moe * n_e, d, two_f)
    moe_w_out = moe_w_out.reshape(n_moe * n_e, two_f // 2, d)

    for i in range(depth):
        last = i == depth - 1
        mod = mods[i]
        rows_in = xs.shape[0]
        rows_out = dm.m_lat if last else dm.m_all
        h = _norm_mod(dm, xs, norm_g[i, 0], mod, 0, rows_in)
        kind, j = i % N_MIXERS, i // N_MIXERS
        if kind == 0:
            qk = diff_w_in.shape[2] // 3
            q = _proj_rope(dm, h, diff_w_in, j, 0, qk, rows_out, tabs, scale=q_scale)
            k = _proj_rope(dm, h, diff_w_in, j, qk, qk, rows_in, tabs)
            v = _proj_plain(dm, h, diff_w_in, j, 2 * qk, qk, rows_in)
            lam_init = 0.8 - 0.6 * math.exp(-0.3 * i)
            y = _attention(dm, q, k, v, kind="diff", ctx_out=not last,
                           params=(diff_lambda[j], diff_subln_g[j].reshape(1, 2 * HEAD_DIM)),
                           lam_init=lam_init)
            w_out = diff_w_out
        elif kind == 1:
            y = _fourier_mix(dm, h, not last)
            w_out = fourier_w_out
        else:
            kvd = (gqa_w_in.shape[2] - d) // 2
            q = _proj_rope(dm, h, gqa_w_in, j, 0, d, rows_out, tabs, scale=q_scale, gain=gqa_qk_g[j, 0])
            k = _proj_rope(dm, h, gqa_w_in, j, d, kvd, rows_in, tabs, gain=gqa_qk_g[j, 1])
            v = _proj_plain(dm, h, gqa_w_in, j, d + kvd, kvd, rows_in)
            y = _attention(dm, q, k, v, kind="gqa", ctx_out=not last)
            w_out = gqa_w_out
        xs = _proj_resid(dm, y, w_out, j, xs, mod, 2, rows_out)

        f = i // 2
        if i % 2 == 0:
            z = _norm_mod(dm, xs, norm_g[i, 1], mod, 3, rows_out)
            hdn = _matmul(dm, z, ffn_w_in, rows=rows_out, mode="swiglu", out_dtype=BF16, e0=f,
                          swiglu_half=ffn_w_in.shape[2] // 2)
            xs = _proj_resid(dm, hdn, ffn_w_out, f, xs, mod, 5, rows_out)
        else:
            z, route, counts = _norm_mod(dm, xs, norm_g[i, 1], mod, 3, rows_out, w_router=moe_router[f])
            slots, src, tile_expert, n_tiles = _route_tables(route, counts, n_e, rows_out)
            zg = _dispatch(z, src, n_tiles)
            hdn = _moe_up(zg, moe_w_in, f * n_e, tile_expert, two_f // 2)
            yg = _moe_down(hdn, moe_w_out, f * n_e, tile_expert)
            xs = _moe_combine(dm, yg, slots, xs, route, mod, 5, rows_out,
                              final_g=final_g if last else None)

    if depth % 2:
        xs = _final_norm(dm, xs, final_g)
    return xs.reshape(b, dm.seq, d)
```
